```python
import jax, jax.numpy as jnp
from jax import lax
import numpy as np

D_MODEL = 1024
BATCH = 8
SEQ = 2048
DEPTH = 1
DEC_BATCH = 32
DEC_SEQ = 1
PAST_LEN = 16384
PAGE_SIZE = 128

HG_HEADS = 4
HG_DK = 128
HG_DV = 128
HG_WIDTH = HG_HEADS * HG_DK
HG_OUT = HG_HEADS * HG_DV
HG_CHUNK = 64
FOX_HEADS = 8
FOX_DH = 64
FOX_WIDTH = FOX_HEADS * FOX_DH
FOX_QBLOCK = 128
FOX_F_BIAS_INIT = 8.0
N_EXPERTS = 32
TOP_K = 4
D_FF = D_MODEL
SWIGLU_ALPHA = 1.702
SWIGLU_LIMIT = 7.0
MOE_BLOCK = 128
PLE_DIM = 256
NORM_EPS = 1e-6
IN_SPLITS = (HG_WIDTH, HG_WIDTH, HG_OUT, HG_OUT, FOX_WIDTH, FOX_WIDTH, FOX_WIDTH, FOX_HEADS)
IN_WIDTH = 2 * HG_WIDTH + 2 * HG_OUT + 3 * FOX_WIDTH + FOX_HEADS

kernel_name = 'hymba_hgrn2_fox_moe_decode_step'


def rmsnorm(x, g):
    xf = x.astype(jnp.float32)
    xf = xf * lax.rsqrt(jnp.mean(xf * xf, axis=-1, keepdims=True) + NORM_EPS)
    return (xf * g.astype(jnp.float32)).astype(x.dtype)


def head_rmsnorm(o, g):
    of = o.astype(jnp.float32)
    of = of * lax.rsqrt(jnp.mean(of * of, axis=-1, keepdims=True) + NORM_EPS)
    return of.reshape(*o.shape[:-2], o.shape[-2] * o.shape[-1]) * g.astype(jnp.float32)


def hgrn2_chunked(q, k, v, log_f, s0):
    B, T, H, Dk = q.shape
    Dv = v.shape[-1]
    C = HG_CHUNK if T % HG_CHUNK == 0 else T
    n = T // C

    def to_chunks(a):
        return a.reshape(B, n, C, H, a.shape[-1]).transpose(1, 0, 3, 2, 4)

    causal = jnp.tril(jnp.ones((C, C), bool))

    def step(S, inp):
        qc, kc, vc, gc = inp
        b = jnp.cumsum(gc, axis=2)
        o_inter = jnp.einsum('bhtk,bhkv->bhtv', qc * jnp.exp(b), S)
        diff = b[:, :, :, None, :] - b[:, :, None, :, :]
        decay = jnp.exp(jnp.where(causal[:, :, None], diff, -jnp.inf))
        A = jnp.einsum('bhtk,bhsk,bhtsk->bhts', qc, kc, decay)
        o_intra = jnp.einsum('bhts,bhsv->bhtv', A, vc)
        b_last = b[:, :, -1:, :]
        S_new = jnp.exp(b_last[:, :, 0, :, None]) * S + jnp.einsum('bhsk,bhsv->bhkv', kc * jnp.exp(b_last - b), vc)
        return S_new, o_inter + o_intra

    S_fin, o = lax.scan(step, s0, (to_chunks(q), to_chunks(k), to_chunks(v), to_chunks(log_f)))
    o = o.transpose(1, 0, 3, 2, 4).reshape(B, T, H, Dv)
    return o, S_fin


def fox_prompt_attn(q, k, v, logf):
    B, T, H, Dh = q.shape
    QB = FOX_QBLOCK if T % FOX_QBLOCK == 0 else T
    nq = T // QB
    scale = FOX_DH ** -0.5
    c = jnp.cumsum(logf, axis=1)
    c_keys = c.transpose(0, 2, 1)
    key_pos = jnp.arange(T)
    qb = q.reshape(B, nq, QB, H, Dh).transpose(1, 0, 2, 3, 4)
    cb = c.reshape(B, nq, QB, H).transpose(1, 0, 3, 2)

    def block(args):
        qi, ci, i = args
        s = jnp.einsum('bqhd,bkhd->bhqk', qi, k, preferred_element_type=jnp.float32) * scale
        s = s + ci[..., None] - c_keys[:, :, None, :]
        q_pos = i * QB + jnp.arange(QB)
        s = jnp.where(key_pos[None, :] <= q_pos[:, None], s, -jnp.inf)
        p = jax.nn.softmax(s, axis=-1)
        return jnp.einsum('bhqk,bkhd->bqhd', p.astype(v.dtype), v)

    o = lax.map(block, (qb, cb, jnp.arange(nq)))
    return o.transpose(1, 0, 2, 3, 4).reshape(B, T, H, Dh)


def fox_decode_attn(q, k, v, logf, k_past, v_past, lf_past):
    P = k_past.shape[1]
    T = q.shape[1]
    scale = FOX_DH ** -0.5
    c_new = jnp.cumsum(logf, axis=1).transpose(0, 2, 1)
    lf_p = lf_past.astype(jnp.float32)
    suffix = (lax.cumsum(lf_p, axis=1, reverse=True) - lf_p).transpose(0, 2, 1)
    s_past = jnp.einsum('bqhd,bkhd->bhqk', q, k_past, preferred_element_type=jnp.float32) * scale
    s_past = s_past + c_new[..., None] + suffix[:, :, None, :]
    s_new = jnp.einsum('bqhd,bkhd->bhqk', q, k, preferred_element_type=jnp.float32) * scale
    s_new = s_new + c_new[..., None] - c_new[:, :, None, :]
    causal = jnp.tril(jnp.ones((T, T), bool))
    s_new = jnp.where(causal, s_new, -jnp.inf)
    p = jax.nn.softmax(jnp.concatenate([s_past, s_new], axis=-1), axis=-1)
    o = jnp.einsum('bhqk,bkhd->bqhd', p[..., :P].astype(v_past.dtype), v_past)
    return o + jnp.einsum('bhqk,bkhd->bqhd', p[..., P:].astype(v.dtype), v)


def moe_ffn(u, w_router, b_router, w_up, b_up, w_down, b_down):
    shape = u.shape
    xt = u.reshape(-1, shape[-1])
    n_tok = xt.shape[0]
    logits = jnp.einsum('nd,de->ne', xt, w_router, preferred_element_type=jnp.float32) + b_router.astype(jnp.float32)
    top_val, top_idx = lax.top_k(logits, TOP_K)
    gates = jax.nn.softmax(top_val, axis=-1)
    n_assign = n_tok * TOP_K
    exp_flat = top_idx.reshape(-1).astype(jnp.int32)
    tok_flat = jnp.repeat(jnp.arange(n_tok, dtype=jnp.int32), TOP_K)
    gate_flat = gates.reshape(-1)
    order = jnp.argsort(exp_flat)
    e_s = exp_flat[order]
    t_s = tok_flat[order]
    g_s = gate_flat[order]
    counts = jnp.zeros((N_EXPERTS,), jnp.int32).at[exp_flat].add(1)
    padded = (counts + MOE_BLOCK - 1) // MOE_BLOCK * MOE_BLOCK
    start = jnp.cumsum(counts) - counts
    pend = jnp.cumsum(padded)
    pstart = pend - padded
    dest = pstart[e_s] + jnp.arange(n_assign, dtype=jnp.int32) - start[e_s]
    n_blocks = -(-n_assign // MOE_BLOCK) + N_EXPERTS
    cap = n_blocks * MOE_BLOCK
    buf_tok = jnp.full((cap,), n_tok, jnp.int32).at[dest].set(t_s)
    buf_gate = jnp.zeros((cap,), jnp.float32).at[dest].set(g_s)
    block_exp = jnp.minimum(jnp.searchsorted(pend, jnp.arange(n_blocks, dtype=jnp.int32) * MOE_BLOCK, side='right'), N_EXPERTS - 1).astype(jnp.int32)
    x_pad = jnp.concatenate([xt, jnp.zeros((1, xt.shape[1]), xt.dtype)], axis=0)
    xb = x_pad[buf_tok].reshape(n_blocks, MOE_BLOCK, xt.shape[1])

    def expert_block(args):
        xblk, e = args
        hdn = xblk @ w_up[e] + b_up[e]
        glu = jnp.minimum(hdn[:, :D_FF], SWIGLU_LIMIT)
        lin = jnp.clip(hdn[:, D_FF:], -SWIGLU_LIMIT, SWIGLU_LIMIT)
        act = glu * jax.nn.sigmoid(SWIGLU_ALPHA * glu) * (lin + 1.0)
        return act @ w_down[e] + b_down[e]

    yb = lax.map(expert_block, (xb, block_exp)).reshape(cap, xt.shape[1])
    y = jax.ops.segment_sum(yb * buf_gate[:, None].astype(yb.dtype), buf_tok, num_segments=n_tok + 1)[:n_tok]
    return y.reshape(shape)


def hybrid_layer(h, p_emb, s0, fox_past, lw):
    B, T, _ = h.shape
    u = rmsnorm(h, lw['g_mix'])
    proj = u @ lw['w_in']
    split_pts = np.cumsum(IN_SPLITS)[:-1].tolist()
    hq, hf, hi, hg, fq, fk, fv, ff = jnp.split(proj, split_pts, axis=-1)
    lb = lw['lb'].reshape(HG_HEADS, HG_DK)
    q_h = jax.nn.silu(hq.astype(jnp.float32)).reshape(B, T, HG_HEADS, HG_DK)
    f_h = lb + (1.0 - lb) * jax.nn.sigmoid(hf.astype(jnp.float32).reshape(B, T, HG_HEADS, HG_DK))
    k_h = 1.0 - f_h
    v_h = hi.astype(jnp.float32).reshape(B, T, HG_HEADS, HG_DV)
    o_h, s_new = hgrn2_chunked(q_h, k_h, v_h, jnp.log(f_h), s0.astype(jnp.float32))
    o_h = head_rmsnorm(o_h, lw['g_hg_out']) * jax.nn.silu(hg.astype(jnp.float32))
    q_f = fq.reshape(B, T, FOX_HEADS, FOX_DH)
    k_f = fk.reshape(B, T, FOX_HEADS, FOX_DH)
    v_f = fv.reshape(B, T, FOX_HEADS, FOX_DH)
    logf = jax.nn.log_sigmoid(ff.astype(jnp.float32) + lw['fox_f_bias'].astype(jnp.float32))
    if fox_past is None:
        o_f = fox_prompt_attn(q_f, k_f, v_f, logf)
    else:
        k_past, v_past, lf_past = fox_past
        o_f = fox_decode_attn(q_f, k_f, v_f, logf, k_past, v_past, lf_past)
    o_f = head_rmsnorm(o_f, lw['g_fox_out'])
    mix = jnp.concatenate([o_h, o_f], axis=-1).astype(h.dtype) @ lw['w_out']
    h = h + mix
    h = h + moe_ffn(rmsnorm(h, lw['g_mlp']), lw['w_router'], lw['b_router'], lw['w_up'], lw['b_up'], lw['w_down'], lw['b_down'])
    h = h + jax.nn.sigmoid(h @ lw['w_ple_gate']) * (p_emb.astype(h.dtype) @ lw['w_ple'])
    return h, s_new.astype(s0.dtype), k_f, v_f, logf


def setup_inputs(seed: int = 0) -> dict:
    key = jax.random.key(seed)
    ks = jax.random.split(key, 26)
    n_pages = PAST_LEN // PAGE_SIZE
    n_used = DEC_BATCH * n_pages
    n_phys = n_used + n_used // 4
    f32 = jnp.float32
    nrm = lambda k, s: jax.random.normal(k, s, f32)
    return {
        'x_prompt': nrm(ks[0], (BATCH, SEQ, D_MODEL)),
        'x_sample': nrm(ks[1], (DEC_BATCH, DEC_SEQ, D_MODEL)),
        'state_hgrn': 0.5 * nrm(ks[2], (DEPTH, DEC_BATCH, HG_HEADS, HG_DK, HG_DV)),
        'cache_k': nrm(ks[3], (DEPTH, n_phys, PAGE_SIZE, FOX_HEADS, FOX_DH)),
        'cache_v': nrm(ks[4], (DEPTH, n_phys, PAGE_SIZE, FOX_HEADS, FOX_DH)),
        'cache_logf': jax.nn.log_sigmoid(FOX_F_BIAS_INIT + nrm(ks[5], (DEPTH, n_phys, PAGE_SIZE, FOX_HEADS))),
        'page_table': jax.random.permutation(ks[6], n_phys)[:n_used].reshape(DEC_BATCH, n_pages).astype(jnp.int32),
        'p_prompt': nrm(ks[7], (DEPTH, BATCH, SEQ, PLE_DIM)),
        'p_sample': nrm(ks[8], (DEPTH, DEC_BATCH, DEC_SEQ, PLE_DIM)),
        'g_mix': 1.0 + 0.05 * nrm(ks[9], (DEPTH, D_MODEL)),
        'w_in': nrm(ks[10], (DEPTH, D_MODEL, IN_WIDTH)) * D_MODEL ** -0.5,
        'hg_lb_logits': 0.1 * nrm(ks[11], (DEPTH + 1, HG_WIDTH)),
        'g_hg_out': 1.0 + 0.05 * nrm(ks[12], (DEPTH, HG_OUT)),
        'fox_f_bias': FOX_F_BIAS_INIT + 0.1 * nrm(ks[13], (DEPTH, FOX_HEADS)),
        'g_fox_out': 1.0 + 0.05 * nrm(ks[14], (DEPTH, FOX_WIDTH)),
        'w_out': nrm(ks[15], (DEPTH, D_MODEL, D_MODEL)) * D_MODEL ** -0.5,
        'g_mlp': 1.0 + 0.05 * nrm(ks[16], (DEPTH, D_MODEL)),
        'w_router': nrm(ks[17], (DEPTH, D_MODEL, N_EXPERTS)) * D_MODEL ** -0.5,
        'b_router': 0.01 * nrm(ks[18], (DEPTH, N_EXPERTS)),
        'w_up': nrm(ks[19], (DEPTH, N_EXPERTS, D_MODEL, 2 * D_FF)) * D_MODEL ** -0.5,
        'b_up': 0.01 * nrm(ks[20], (DEPTH, N_EXPERTS, 2 * D_FF)),
        'w_down': nrm(ks[21], (DEPTH, N_EXPERTS, D_FF, D_MODEL)) * D_FF ** -0.5,
        'b_down': 0.01 * nrm(ks[22], (DEPTH, N_EXPERTS, D_MODEL)),
        'w_ple': nrm(ks[23], (DEPTH, PLE_DIM, D_MODEL)) * PLE_DIM ** -0.5,
        'w_ple_gate': nrm(ks[24], (DEPTH, D_MODEL, D_MODEL)) * D_MODEL ** -0.5,
        'g_final': 1.0 + 0.05 * nrm(ks[25], (D_MODEL,)),
    }


def reference(x_prompt, x_sample, state_hgrn, cache_k, cache_v, cache_logf, page_table, p_prompt, p_sample,
              g_mix, w_in, hg_lb_logits, g_hg_out, fox_f_bias, g_fox_out, w_out, g_mlp, w_router, b_router,
              w_up, b_up, w_down, b_down, w_ple, w_ple_gate, g_final):
    dec_b, n_pages = page_table.shape
    past = n_pages * PAGE_SIZE
    lb_all = jnp.cumsum(jax.nn.softmax(hg_lb_logits.astype(jnp.float32), axis=0), axis=0)
    hp, hs = x_prompt, x_sample
    sp_l, kp_l, vp_l, lp_l, ss_l, ks_l, vs_l, ls_l = [], [], [], [], [], [], [], []
    for i in range(DEPTH):
        lw = dict(g_mix=g_mix[i], w_in=w_in[i], lb=lb_all[i], g_hg_out=g_hg_out[i], fox_f_bias=fox_f_bias[i],
                  g_fox_out=g_fox_out[i], w_out=w_out[i], g_mlp=g_mlp[i], w_router=w_router[i],
                  b_router=b_router[i], w_up=w_up[i], b_up=b_up[i], w_down=w_down[i], b_down=b_down[i],
                  w_ple=w_ple[i], w_ple_gate=w_ple_gate[i])
        s0p = jnp.zeros((x_prompt.shape[0], HG_HEADS, HG_DK, HG_DV), state_hgrn.dtype)
        hp, sp, kp, vp, lp = hybrid_layer(hp, p_prompt[i], s0p, None, lw)
        k_past = cache_k[i, page_table].reshape(dec_b, past, FOX_HEADS, FOX_DH)
        v_past = cache_v[i, page_table].reshape(dec_b, past, FOX_HEADS, FOX_DH)
        lf_past = cache_logf[i, page_table].reshape(dec_b, past, FOX_HEADS)
        hs, ss, ks_, vs, ls = hybrid_layer(hs, p_sample[i], state_hgrn[i], (k_past, v_past, lf_past), lw)
        sp_l.append(sp); kp_l.append(kp); vp_l.append(vp); lp_l.append(lp)
        ss_l.append(ss); ks_l.append(ks_); vs_l.append(vs); ls_l.append(ls)
    y_prompt = rmsnorm(hp, g_final)
    y_sample = rmsnorm(hs, g_final)
    return (y_prompt, y_sample, jnp.stack(sp_l), jnp.stack(kp_l), jnp.stack(vp_l), jnp.stack(lp_l),
            jnp.stack(ss_l), jnp.stack(ks_l), jnp.stack(vs_l), jnp.stack(ls_l))
```

```python
import functools

import jax
import jax.numpy as jnp
from jax import lax
from jax.experimental import pallas as pl
from jax.experimental.pallas import tpu as pltpu

F32 = jnp.float32
BF16 = jnp.bfloat16

HG_HEADS = 4
HG_DK = 128
HG_DV = 128
HG_WIDTH = HG_HEADS * HG_DK
FOX_HEADS = 8
FOX_DH = 64
FOX_WIDTH = FOX_HEADS * FOX_DH
N_EXPERTS = 32
TOP_K = 4
PAGE_SIZE = 128
NORM_EPS = 1e-6
SWIGLU_ALPHA = 1.702
SWIGLU_LIMIT = 7.0

LANES = 128
SUBLANES = 8
NEG_BIG = -1e30
VMEM_LIMIT = 56 * 1024 * 1024

HG_CHUNK = 16
HG_BLOCK = 128


def _params(*sem):
    return pltpu.CompilerParams(dimension_semantics=sem, vmem_limit_bytes=VMEM_LIMIT)


def _row_tile(m, pref):
    return pref if m % pref == 0 else m


def _split3(x):
    hi = x.astype(BF16)
    r1 = x - hi.astype(F32)
    mid = r1.astype(BF16)
    lo = (r1 - mid.astype(F32)).astype(BF16)
    return hi, mid, lo


def _dot01(mask_bf16, x):
    hi, mid, lo = _split3(x)
    d = lambda a: jnp.dot(mask_bf16, a, preferred_element_type=F32)
    return d(hi) + d(mid) + d(lo)


def _tri_lower(n, strict=False):
    r = lax.broadcasted_iota(jnp.int32, (n, n), 0)
    c = lax.broadcasted_iota(jnp.int32, (n, n), 1)
    m = (c < r) if strict else (c <= r)
    return jnp.where(m, 1.0, 0.0).astype(BF16)


def _sigmoid(x):
    return 1.0 / (1.0 + jnp.exp(-x))


def _silu(x):
    return x * _sigmoid(x)


def _log_sigmoid(x):
    return jnp.minimum(x, 0.0) - jnp.log1p(jnp.exp(-jnp.abs(x)))


def _inproj_kernel(x_ref, g_ref, w_ref, wf_ref, lb_ref, fb_ref,
                   q_ref, f_ref, i_ref, gt_ref, fq_ref, fk_ref, fv_ref, kb_ref, vb_ref, lf_ref, c_ref,
                   carry_ref, *, tiles_per_seq):
    x = x_ref[...]
    ms = jnp.mean(x * x, axis=-1, keepdims=True)
    u = (x * lax.rsqrt(ms + NORM_EPS) * g_ref[...]).astype(BF16)

    def proj(j):
        return jnp.dot(u, w_ref[:, j * HG_WIDTH:(j + 1) * HG_WIDTH], preferred_element_type=F32)

    q_ref[...] = _silu(proj(0))
    lb = lb_ref[...]
    f_ref[...] = lb + (1.0 - lb) * _sigmoid(proj(1))
    i_ref[...] = proj(2)
    gt_ref[...] = _silu(proj(3))
    fq_ref[...] = (proj(4) * (FOX_DH ** -0.5)).astype(BF16)
    fk = proj(5)
    fk_ref[...] = fk
    kb_ref[...] = fk.astype(BF16)
    fv = proj(6)
    fv_ref[...] = fv
    vb_ref[...] = fv.astype(BF16)
    ff = jnp.dot(u, wf_ref[...], preferred_element_type=F32) + fb_ref[...]
    lf = _log_sigmoid(ff)
    lf_ref[...] = lf[:, :FOX_HEADS]
    if tiles_per_seq == 0:
        c_ref[...] = lf[:, :FOX_HEADS]
    else:
        tm = x.shape[0]

        @pl.when(pl.program_id(0) % tiles_per_seq == 0)
        def _():
            carry_ref[...] = jnp.zeros_like(carry_ref)

        c = _dot01(_tri_lower(tm), lf) + carry_ref[0:1, :]
        carry_ref[...] = jnp.broadcast_to(c[tm - 1:tm, :], carry_ref.shape)
        c_ref[...] = c[:, :FOX_HEADS]


def _inproj(x2d, g_mix, w_main, w_ff, lb, f_bias, seq_len):
    m, d = x2d.shape
    tm = _row_tile(m if seq_len == 1 else seq_len, 512)
    tiles_per_seq = 0 if seq_len == 1 else seq_len // tm
    row = lambda w: pl.BlockSpec((tm, w), lambda i: (i, 0))
    full = lambda a: pl.BlockSpec(a.shape, lambda i: (0,) * a.ndim)
    wide = jax.ShapeDtypeStruct((m, HG_WIDTH), F32)
    wide_bf = jax.ShapeDtypeStruct((m, HG_WIDTH), BF16)
    narrow = jax.ShapeDtypeStruct((m, FOX_HEADS), F32)
    return pl.pallas_call(
        functools.partial(_inproj_kernel, tiles_per_seq=tiles_per_seq),
        grid=(m // tm,),
        in_specs=[row(d), full(g_mix), full(w_main), full(w_ff), full(lb), full(f_bias)],
        out_specs=[row(HG_WIDTH)] * 9 + [row(FOX_HEADS)] * 2,
        out_shape=[wide, wide, wide, wide, wide_bf, wide, wide, wide_bf, wide_bf, narrow, narrow],
        scratch_shapes=[pltpu.VMEM((SUBLANES, LANES), F32)],
        compiler_params=_params("arbitrary"),
        name="inproj",
    )(x2d, g_mix, w_main, w_ff, lb, f_bias)


def _hgrn_kernel(q_ref, f_ref, v_ref, gt_ref, gn_ref, o_ref, s_ref, s_scr, b_scr):
    tb = q_ref.shape[0]

    @pl.when(pl.program_id(1) == 0)
    def _():
        s_scr[...] = jnp.zeros_like(s_scr)

    tri = _tri_lower(HG_BLOCK)
    row_id = lax.broadcasted_iota(jnp.int32, (HG_CHUNK, 1), 0)

    def block_body(blk, carry):
        r_blk = pl.multiple_of(blk * HG_BLOCK, HG_BLOCK)
        b_scr[...] = _dot01(tri, jnp.log(f_ref[pl.ds(r_blk, HG_BLOCK), :]))

        def chunk_body(ci, carry2):
            r_loc = pl.multiple_of(ci * HG_CHUNK, HG_CHUNK)
            r0 = r_blk + r_loc
            for h in range(HG_HEADS):
                cols = slice(h * HG_DK, (h + 1) * HG_DK)
                q = q_ref[pl.ds(r0, HG_CHUNK), cols]
                k = 1.0 - f_ref[pl.ds(r0, HG_CHUNK), cols]
                v = v_ref[pl.ds(r0, HG_CHUNK), cols]
                bb = b_scr[pl.ds(r_loc, HG_CHUNK), cols]
                b_prev = bb[0:1, :] - jnp.log(f_ref[pl.ds(r0, 1), cols])
                b = bb - b_prev
                s_old = s_scr[h]
                o = jnp.dot((q * jnp.exp(b)).astype(BF16), s_old.astype(BF16), preferred_element_type=F32)
                for s in range(HG_CHUNK):
                    d = jnp.where(row_id >= s, b - b[s:s + 1, :], NEG_BIG)
                    w = jnp.exp(d) * q * k[s:s + 1, :]
                    o = o + jnp.sum(w, axis=-1, keepdims=True) * v[s:s + 1, :]
                b_last = b[HG_CHUNK - 1:HG_CHUNK, :]
                kd = (k * jnp.exp(b_last - b)).astype(BF16)
                upd = lax.dot_general(kd, v.astype(BF16), (((0,), (0,)), ((), ())),
                                      preferred_element_type=F32)
                eb = jnp.transpose(jnp.broadcast_to(jnp.exp(b_last), (HG_DK, HG_DK)))
                s_scr[h] = eb * s_old + upd
                ms = jnp.mean(o * o, axis=-1, keepdims=True)
                o = o * lax.rsqrt(ms + NORM_EPS) * gn_ref[:, cols] * gt_ref[pl.ds(r0, HG_CHUNK), cols]
                o_ref[pl.ds(r0, HG_CHUNK), cols] = o.astype(BF16)
            return carry2

        return lax.fori_loop(0, HG_BLOCK // HG_CHUNK, chunk_body, carry)

    lax.fori_loop(0, tb // HG_BLOCK, block_body, 0)

    @pl.when(pl.program_id(1) == pl.num_programs(1) - 1)
    def _():
        s_ref[0] = s_scr[...]


def _hgrn_prompt(q, f, v, gt, g_norm, batch, seq):
    tb = _row_tile(seq, 512)
    assert tb % HG_BLOCK == 0
    nt = seq // tb
    row = pl.BlockSpec((tb, HG_WIDTH), lambda b, t: (b * nt + t, 0))
    return pl.pallas_call(
        _hgrn_kernel,
        grid=(batch, nt),
        in_specs=[row, row, row, row, pl.BlockSpec((1, HG_WIDTH), lambda b, t: (0, 0))],
        out_specs=[row, pl.BlockSpec((1, HG_HEADS, HG_DK, HG_DV), lambda b, t: (b, 0, 0, 0))],
        out_shape=[jax.ShapeDtypeStruct((batch * seq, HG_WIDTH), BF16),
                   jax.ShapeDtypeStruct((batch, HG_HEADS, HG_DK, HG_DV), F32)],
        scratch_shapes=[pltpu.VMEM((HG_HEADS, HG_DK, HG_DV), F32), pltpu.VMEM((HG_BLOCK, HG_WIDTH), F32)],
        compiler_params=_params("arbitrary", "arbitrary"),
        name="hgrn_prompt",
    )(q, f, v, gt, g_norm)


def _fox_prompt_kernel(q_ref, k_ref, v_ref, cq_ref, ck_ref, gn_ref, o_ref, m_scr, l_scr, acc_scr):
    pair = pl.program_id(1)
    qi = pl.program_id(2)
    ki = pl.program_id(3)
    tq = q_ref.shape[0]
    tk = k_ref.shape[0]

    @pl.when(ki == 0)
    def _():
        m_scr[...] = jnp.full_like(m_scr, NEG_BIG)
        l_scr[...] = jnp.zeros_like(l_scr)
        acc_scr[...] = jnp.zeros_like(acc_scr)

    @pl.when(ki <= qi)
    def _():
        lane = lax.broadcasted_iota(jnp.int32, (1, LANES), 1)
        first = lane < FOX_DH
        q = q_ref[...]
        k = k_ref[...]
        v = v_ref[...]
        q_pos = qi * tq + lax.broadcasted_iota(jnp.int32, (tq, tk), 0)
        k_pos = ki * tk + lax.broadcasted_iota(jnp.int32, (tq, tk), 1)
        causal = k_pos <= q_pos
        cq_all = cq_ref[...]
        ck_all = ck_ref[...]
        head_col = lax.broadcasted_iota(jnp.int32, (1, FOX_HEADS), 1)
        head_row = lax.broadcasted_iota(jnp.int32, (FOX_HEADS, 1), 0)
        pv = []
        for j in range(2):
            h = 2 * pair + j
            qh = jnp.where(first if j == 0 else ~first, q, jnp.zeros_like(q))
            s = lax.dot_general(qh, k, (((1,), (1,)), ((), ())), preferred_element_type=F32)
            cq = jnp.sum(jnp.where(head_col == h, cq_all, 0.0), axis=1, keepdims=True)
            ck = jnp.sum(jnp.where(head_row == h, ck_all, 0.0), axis=0, keepdims=True)
            s = jnp.where(causal, s + cq - ck, NEG_BIG)
            m_old = m_scr[j]
            m_new = jnp.maximum(m_old, jnp.max(s, axis=-1, keepdims=True))
            alpha = jnp.exp(m_old - m_new)
            p = jnp.exp(s - m_new)
            l_scr[j] = alpha * l_scr[j] + jnp.sum(p, axis=-1, keepdims=True)
            m_scr[j] = m_new
            pv.append((alpha, jnp.dot(p.astype(BF16), v, preferred_element_type=F32)))
        alpha = jnp.where(first, pv[0][0], pv[1][0])
        acc_scr[...] = alpha * acc_scr[...] + jnp.where(first, pv[0][1], pv[1][1])

    @pl.when(ki == qi)
    def _():
        lane = lax.broadcasted_iota(jnp.int32, (1, LANES), 1)
        first = lane < FOX_DH
        o = acc_scr[...] / jnp.where(first, l_scr[0], l_scr[1])
        sq = o * o
        ss0 = jnp.sum(jnp.where(first, sq, 0.0), axis=-1, keepdims=True)
        ss1 = jnp.sum(jnp.where(first, 0.0, sq), axis=-1, keepdims=True)
        ms = jnp.where(first, ss0, ss1) * (1.0 / FOX_DH)
        o_ref[...] = (o * lax.rsqrt(ms + NORM_EPS) * gn_ref[...]).astype(BF16)


def _fox_prompt(q_bf, k_bf, v_bf, c_col, c_row, g_norm, batch, seq):
    tq = _row_tile(seq, 512)
    nq = seq // tq
    pairs = FOX_HEADS // 2
    kv_map = lambda b, p, i, j: (b * nq + jnp.minimum(j, i), p)
    return pl.pallas_call(
        _fox_prompt_kernel,
        grid=(batch, pairs, nq, nq),
        in_specs=[pl.BlockSpec((tq, LANES), lambda b, p, i, j: (b * nq + i, p)),
                  pl.BlockSpec((tq, LANES), kv_map),
                  pl.BlockSpec((tq, LANES), kv_map),
                  pl.BlockSpec((tq, FOX_HEADS), lambda b, p, i, j: (b * nq + i, 0)),
                  pl.BlockSpec((FOX_HEADS, tq), lambda b, p, i, j: (0, b * nq + jnp.minimum(j, i))),
                  pl.BlockSpec((1, LANES), lambda b, p, i, j: (0, p))],
        out_specs=pl.BlockSpec((tq, LANES), lambda b, p, i, j: (b * nq + i, p)),
        out_shape=jax.ShapeDtypeStruct((batch * seq, FOX_WIDTH), BF16),
        scratch_shapes=[pltpu.VMEM((2, tq, 1), F32), pltpu.VMEM((2, tq, 1), F32), pltpu.VMEM((tq, LANES), F32)],
        compiler_params=_params("arbitrary", "arbitrary", "arbitrary", "arbitrary"),
        name="fox_prompt",
    )(q_bf, k_bf, v_bf, c_col, c_row, g_norm)


def _hgrn_step_kernel(q_ref, f_ref, v_ref, gt_ref, gn_ref, s_ref, o_ref, so_ref):
    rows = q_ref.shape[0]

    def col(row):
        return jnp.transpose(jnp.broadcast_to(row, (HG_DK, HG_DK)))

    for r in range(rows):
        for h in range(HG_HEADS):
            cols = slice(h * HG_DK, (h + 1) * HG_DK)
            f_col = col(f_ref[r:r + 1, cols])
            q_col = col(q_ref[r:r + 1, cols])
            s_new = f_col * s_ref[r, h] + (1.0 - f_col) * v_ref[r:r + 1, cols]
            so_ref[r, h] = s_new
            o = jnp.sum(s_new * q_col, axis=0, keepdims=True)
            ms = jnp.mean(o * o, axis=-1, keepdims=True)
            o = o * lax.rsqrt(ms + NORM_EPS) * gn_ref[:, cols] * gt_ref[r:r + 1, cols]
            o_ref[r:r + 1, cols] = o.astype(BF16)


def _hgrn_step(q, f, v, gt, g_norm, state):
    n = q.shape[0]
    rows = SUBLANES
    assert n % rows == 0
    row = pl.BlockSpec((rows, HG_WIDTH), lambda i: (i, 0))
    st = pl.BlockSpec((rows, HG_HEADS, HG_DK, HG_DV), lambda i: (i, 0, 0, 0))
    return pl.pallas_call(
        _hgrn_step_kernel,
        grid=(n // rows,),
        in_specs=[row, row, row, row, pl.BlockSpec((1, HG_WIDTH), lambda i: (0, 0)), st],
        out_specs=[row, st],
        out_shape=[jax.ShapeDtypeStruct((n, HG_WIDTH), BF16), jax.ShapeDtypeStruct(state.shape, F32)],
        compiler_params=_params("arbitrary"),
        name="hgrn_step",
    )(q, f, v, gt, g_norm, state)


FOX_PAGES_PER_STEP = 8


def _seg_expand_matrix():
    h = lax.broadcasted_iota(jnp.int32, (FOX_HEADS, FOX_WIDTH), 0)
    lane = lax.broadcasted_iota(jnp.int32, (FOX_HEADS, FOX_WIDTH), 1)
    return jnp.where(lane // FOX_DH == h, 1.0, 0.0).astype(BF16)


def _expand_heads(x, e):
    return _dot01_rhs(jnp.broadcast_to(x, (SUBLANES, FOX_HEADS)), e)[0:1, :]


def _dot01_rhs(x, mask_bf16):
    hi, mid, lo = _split3(x)
    d = lambda a: jnp.dot(a, mask_bf16, preferred_element_type=F32)
    return d(hi) + d(mid) + d(lo)


def _seg_sum(x, e):
    hi, mid, lo = _split3(x)
    d = lambda a: lax.dot_general(a, e, (((1,), (1,)), ((), ())), preferred_element_type=F32)
    return d(hi) + d(mid) + d(lo)


def _fox_decode_kernel(pt_ref, q_ref, kn_ref, vn_ref, lfn_ref, gn_ref, *refs):
    del pt_ref
    npg = FOX_PAGES_PER_STEP
    k_refs, v_refs, lf_refs = refs[:npg], refs[npg:2 * npg], refs[2 * npg:3 * npg]
    o_ref, m_scr, l_scr, r_scr, acc_scr = refs[3 * npg:]
    c = pl.program_id(1)
    e = _seg_expand_matrix()
    q = q_ref[0]
    lf_new = lfn_ref[0]

    @pl.when(c == 0)
    def _():
        prod = jnp.broadcast_to(q.astype(F32) * kn_ref[0].astype(BF16).astype(F32), (SUBLANES, FOX_WIDTH))
        m_scr[...] = _seg_sum(prod, e)
        l_scr[...] = jnp.ones_like(l_scr)
        r_scr[...] = jnp.zeros_like(r_scr)
        row = lax.broadcasted_iota(jnp.int32, (SUBLANES, FOX_WIDTH), 0)
        acc_scr[...] = jnp.where(row == 0, jnp.broadcast_to(vn_ref[0], (SUBLANES, FOX_WIDTH)), 0.0)

    q_t = (e.astype(F32) * q.astype(F32)).astype(BF16)
    rr = lax.broadcasted_iota(jnp.int32, (PAGE_SIZE, PAGE_SIZE), 0)
    cc = lax.broadcasted_iota(jnp.int32, (PAGE_SIZE, PAGE_SIZE), 1)
    later = jnp.where(cc > rr, 1.0, 0.0).astype(BF16)

    run = r_scr[0:1, :]
    scores = []
    for g in range(npg):
        k_bf = k_refs[g][0].astype(BF16)
        s = lax.dot_general(k_bf, q_t, (((1,), (1,)), ((), ())), preferred_element_type=F32)
        lf = lf_refs[g][0]
        scores.append(s + _dot01(later, lf) + run + lf_new)
        run = run + jnp.sum(lf, axis=0, keepdims=True)
    r_scr[...] = jnp.broadcast_to(run, r_scr.shape)

    m_old = m_scr[0:1, :]
    m_new = m_old
    for s in scores:
        m_new = jnp.maximum(m_new, jnp.max(s, axis=0, keepdims=True))
    alpha = jnp.exp(m_old - m_new)
    l_new = alpha * l_scr[0:1, :]
    part = jnp.zeros((SUBLANES, FOX_WIDTH), F32)
    for g in range(npg):
        p = jnp.exp(scores[g] - m_new)
        l_new = l_new + jnp.sum(p, axis=0, keepdims=True)
        p_wide = jnp.dot(p.astype(BF16), e, preferred_element_type=F32)
        pv = p_wide * v_refs[g][0]
        part = part + jnp.sum(pv.reshape(PAGE_SIZE // SUBLANES, SUBLANES, FOX_WIDTH), axis=0)
    m_scr[...] = jnp.broadcast_to(m_new, m_scr.shape)
    l_scr[...] = jnp.broadcast_to(l_new, l_scr.shape)
    acc_scr[...] = _expand_heads(alpha, e) * acc_scr[...] + part

    @pl.when(c == pl.num_programs(1) - 1)
    def _():
        o = jnp.sum(acc_scr[...], axis=0, keepdims=True) / _expand_heads(l_scr[0:1, :], e)
        ss = _seg_sum(jnp.broadcast_to(o * o, (SUBLANES, FOX_WIDTH)), e)[0:1, :] * (1.0 / FOX_DH)
        o_ref[0] = (o * lax.rsqrt(_expand_heads(ss, e) + NORM_EPS) * gn_ref[...]).astype(BF16)


def _fox_decode(q_bf, k_new, v_new, lf_new, g_norm, cache_k, cache_v, cache_lf, page_table):
    n, n_pages = page_table.shape
    npg = FOX_PAGES_PER_STEP
    assert n_pages % npg == 0
    n_chunks = n_pages // npg

    def page_map(g):
        return lambda b, c, pt: (pt[b, (n_chunks - 1 - c) * npg + (npg - 1 - g)], 0, 0)

    row3 = lambda w: pl.BlockSpec((1, 1, w), lambda b, c, pt: (b, 0, 0))
    kv_specs = [pl.BlockSpec((1, PAGE_SIZE, FOX_WIDTH), page_map(g)) for g in range(npg)]
    lf_specs = [pl.BlockSpec((1, PAGE_SIZE, FOX_HEADS), page_map(g)) for g in range(npg)]
    grid_spec = pltpu.PrefetchScalarGridSpec(
        num_scalar_prefetch=1,
        grid=(n, n_chunks),
        in_specs=[row3(FOX_WIDTH), row3(FOX_WIDTH), row3(FOX_WIDTH), row3(FOX_HEADS),
                  pl.BlockSpec((1, FOX_WIDTH), lambda b, c, pt: (0, 0))] + kv_specs + kv_specs + lf_specs,
        out_specs=row3(FOX_WIDTH),
        scratch_shapes=[pltpu.VMEM((SUBLANES, FOX_HEADS), F32), pltpu.VMEM((SUBLANES, FOX_HEADS), F32),
                        pltpu.VMEM((SUBLANES, FOX_HEADS), F32), pltpu.VMEM((SUBLANES, FOX_WIDTH), F32)],
    )
    out = pl.pallas_call(
        _fox_decode_kernel,
        grid_spec=grid_spec,
        out_shape=jax.ShapeDtypeStruct((n, 1, FOX_WIDTH), BF16),
        compiler_params=_params("arbitrary", "arbitrary"),
        name="fox_decode",
    )(page_table, q_bf.reshape(n, 1, FOX_WIDTH), k_new.reshape(n, 1, FOX_WIDTH), v_new.reshape(n, 1, FOX_WIDTH),
      lf_new.reshape(n, 1, FOX_HEADS), g_norm, *([cache_k] * npg), *([cache_v] * npg), *([cache_lf] * npg))
    return out.reshape(n, FOX_WIDTH)


def _outproj_kernel(oh_ref, of_ref, x_ref, w_ref, g_ref, wr_ref, br_ref, h_ref, u_ref, idx_ref, gate_ref):
    mix_in = jnp.concatenate([oh_ref[...], of_ref[...]], axis=-1)
    h = x_ref[...] + jnp.dot(mix_in, w_ref[...], preferred_element_type=F32)
    h_ref[...] = h
    ms = jnp.mean(h * h, axis=-1, keepdims=True)
    u = h * lax.rsqrt(ms + NORM_EPS) * g_ref[...]
    u_ref[...] = u
    vals = lax.dot_general(wr_ref[...], u, (((1,), (1,)), ((), ())), preferred_element_type=F32,
                           precision=lax.Precision.HIGHEST) + br_ref[...]
    expert = lax.broadcasted_iota(jnp.int32, vals.shape, 0)
    top_v, top_i = [], []
    for _ in range(TOP_K):
        m = jnp.max(vals, axis=0, keepdims=True)
        sel = jnp.min(jnp.where(vals == m, expert, N_EXPERTS), axis=0, keepdims=True)
        vals = jnp.where(expert == sel, -jnp.inf, vals)
        top_v.append(m)
        top_i.append(sel)
    ex = [jnp.exp(t - top_v[0]) for t in top_v]
    den = ex[0] + ex[1] + ex[2] + ex[3]
    idx_ref[...] = jnp.concatenate(top_i, axis=0)
    gate_ref[...] = jnp.concatenate([x / den for x in ex], axis=0)


def _outproj(oh, of, x, w_out, g_mlp, w_router_t, b_router):
    m, d = x.shape
    tm = _row_tile(m, 512)
    row = lambda w: pl.BlockSpec((tm, w), lambda i: (i, 0))
    full = lambda a: pl.BlockSpec(a.shape, lambda i: (0,) * a.ndim)
    colblk = pl.BlockSpec((TOP_K, tm), lambda i: (0, i))
    return pl.pallas_call(
        _outproj_kernel,
        grid=(m // tm,),
        in_specs=[row(HG_WIDTH), row(FOX_WIDTH), row(d), full(w_out), full(g_mlp), full(w_router_t),
                  full(b_router)],
        out_specs=[row(d), row(d), colblk, colblk],
        out_shape=[jax.ShapeDtypeStruct((m, d), F32), jax.ShapeDtypeStruct((m, d), F32),
                   jax.ShapeDtypeStruct((TOP_K, m), jnp.int32), jax.ShapeDtypeStruct((TOP_K, m), F32)],
        compiler_params=_params("arbitrary"),
        name="outproj",
    )(oh, of, x, w_out, g_mlp, w_router_t, b_router)


MOE_TILE = 512
MOE_BLK = 256


def _moe_plan_kernel(idx_ref, dest_ref, be_ref, nu_ref, cnt_ref, ps_ref, cnt_scr, run_scr, ps_scr):
    phase = pl.program_id(0)
    step = pl.program_id(1)
    tm = idx_ref.shape[1]
    expert = lax.broadcasted_iota(jnp.int32, (N_EXPERTS, tm), 0)
    idx = idx_ref[...]
    onehot = [expert == idx[j:j + 1, :] for j in range(TOP_K)]

    @pl.when((phase == 0) & (step == 0))
    def _():
        cnt_scr[...] = jnp.zeros_like(cnt_scr)

    @pl.when(phase == 0)
    def _():
        tot = jnp.zeros((N_EXPERTS, 1), F32)
        for oh in onehot:
            tot = tot + jnp.sum(jnp.where(oh, 1.0, 0.0), axis=1, keepdims=True)
        cnt_scr[...] = cnt_scr[...] + tot

    @pl.when((phase == 1) & (step == 0))
    def _():
        cnt = cnt_scr[...]
        padded = jnp.floor((cnt + (MOE_BLK - 1)) * (1.0 / MOE_BLK)) * MOE_BLK
        pstart = _dot01(_tri_lower(N_EXPERTS, strict=True), padded)
        pend = pstart + padded
        ps_scr[...] = pstart
        run_scr[...] = jnp.zeros_like(run_scr)
        cnt_ref[...] = cnt.astype(jnp.int32)
        ps_ref[...] = pstart.astype(jnp.int32)
        nb = be_ref.shape[1]
        row0 = lax.broadcasted_iota(jnp.int32, (1, nb), 1).astype(F32) * MOE_BLK
        be = jnp.sum(jnp.where(pend[:, 0:1] <= row0, 1.0, 0.0), axis=0, keepdims=True)
        be_ref[...] = jnp.minimum(be, N_EXPERTS - 1).astype(jnp.int32)
        total = jnp.max(pend[:, 0:1], axis=0, keepdims=True)
        nu_ref[...] = jnp.broadcast_to(total * (1.0 / MOE_BLK), nu_ref.shape).astype(jnp.int32)

    @pl.when(phase == 1)
    def _():
        r = lax.broadcasted_iota(jnp.int32, (tm, tm), 0)
        c = lax.broadcasted_iota(jnp.int32, (tm, tm), 1)
        before = jnp.where(r < c, 1.0, 0.0).astype(BF16)
        run = run_scr[:, 0:1]
        base = ps_scr[:, 0:1]
        rows = []
        for oh in onehot:
            ohf = jnp.where(oh, 1.0, 0.0)
            prefix = jnp.dot(ohf.astype(BF16), before, preferred_element_type=F32)
            rows.append(jnp.sum(ohf * (base + run + prefix), axis=0, keepdims=True))
            run = run + jnp.sum(ohf, axis=1, keepdims=True)
        run_scr[...] = jnp.broadcast_to(run, run_scr.shape)
        dest_ref[...] = jnp.concatenate(rows, axis=0).astype(jnp.int32)


def _moe_blocks(n_tok):
    return n_tok * TOP_K // MOE_BLK + N_EXPERTS


def _moe_plan(idx_all):
    n_tok = idx_all.shape[1]
    nb = -(-_moe_blocks(n_tok) // LANES) * LANES
    lane_i32 = jax.ShapeDtypeStruct((N_EXPERTS, LANES), jnp.int32)
    const = lambda shape: pl.BlockSpec(shape, lambda p, s: (0, 0))
    dest, be, nu, cnt, ps = pl.pallas_call(
        _moe_plan_kernel,
        grid=(2, n_tok // MOE_TILE),
        in_specs=[pl.BlockSpec((TOP_K, MOE_TILE), lambda p, s: (0, s))],
        out_specs=[pl.BlockSpec((TOP_K, MOE_TILE), lambda p, s: (0, s * p)),
                   const((1, nb)), const((1, LANES)), const((N_EXPERTS, LANES)), const((N_EXPERTS, LANES))],
        out_shape=[jax.ShapeDtypeStruct((TOP_K, n_tok), jnp.int32), jax.ShapeDtypeStruct((1, nb), jnp.int32),
                   jax.ShapeDtypeStruct((1, LANES), jnp.int32), lane_i32, lane_i32],
        scratch_shapes=[pltpu.VMEM((N_EXPERTS, LANES), F32)] * 3,
        compiler_params=_params("arbitrary", "arbitrary"),
        name="moe_plan",
    )(idx_all)
    return dest, be[0], nu[0, :1], cnt[:, 0], ps[:, 0]


def _row_copy(src, dst, sem):
    return pltpu.make_async_copy(src, dst, sem)


def _moe_dispatch_kernel(cnt_ref, ps_ref, dest_ref, up_ref, us_ref, xs_ref, zero_scr, sem, *, n_prompt_tiles):
    i = pl.program_id(0)

    def scatter(src_ref):
        n = src_ref.shape[0]

        def start(t, carry):
            for j in range(TOP_K):
                _row_copy(src_ref.at[pl.ds(t, 1)], xs_ref.at[pl.ds(dest_ref[j, t], 1)], sem).start()
            return carry

        def wait(t, carry):
            for j in range(TOP_K):
                _row_copy(src_ref.at[pl.ds(0, 1)], xs_ref.at[pl.ds(0, 1)], sem).wait()
            return carry

        lax.fori_loop(0, n, start, 0)
        lax.fori_loop(0, n, wait, 0)

    @pl.when(i < n_prompt_tiles)
    def _():
        scatter(up_ref)

    @pl.when(i == n_prompt_tiles)
    def _():
        scatter(us_ref)

    @pl.when(i == n_prompt_tiles + 1)
    def _():
        zero_scr[...] = jnp.zeros_like(zero_scr)

        def fill(e, start_not_wait):
            cnt = cnt_ref[e]
            pad = (-cnt) & (MOE_BLK - 1)
            off = ps_ref[e] + cnt

            def one(r, carry):
                cp = _row_copy(zero_scr.at[pl.ds(0, 1)], xs_ref.at[pl.ds(off + r, 1)], sem)
                if start_not_wait:
                    cp.start()
                else:
                    cp.wait()
                return carry

            lax.fori_loop(0, pad, one, 0)

        lax.fori_loop(0, N_EXPERTS, lambda e, c: (fill(e, True), c)[1], 0)
        lax.fori_loop(0, N_EXPERTS, lambda e, c: (fill(e, False), c)[1], 0)

        last = N_EXPERTS - 1
        used = (ps_ref[last] + cnt_ref[last] + (MOE_BLK - 1)) // MOE_BLK
        n_blocks = xs_ref.shape[0] // MOE_BLK

        def tail(start_not_wait):
            def one(b, carry):
                row0 = pl.multiple_of(b * MOE_BLK, MOE_BLK)
                for r in range(0, MOE_BLK, SUBLANES):
                    cp = _row_copy(zero_scr, xs_ref.at[pl.ds(row0 + r, SUBLANES)], sem)
                    if start_not_wait:
                        cp.start()
                    else:
                        cp.wait()
                return carry

            lax.fori_loop(used, n_blocks, one, 0)

        tail(True)
        tail(False)


def _moe_dispatch(u_prompt, u_sample, dest, cnt, pstart):
    n_p, d = u_prompt.shape
    n_s = u_sample.shape[0]
    n_tok = dest.shape[1]
    assert n_p % MOE_TILE == 0 and n_s <= MOE_TILE and n_tok == n_p + MOE_TILE
    npt = n_p // MOE_TILE
    cap = _moe_blocks(n_tok) * MOE_BLK
    grid_spec = pltpu.PrefetchScalarGridSpec(
        num_scalar_prefetch=2,
        grid=(npt + 2,),
        in_specs=[pl.BlockSpec((TOP_K, MOE_TILE), lambda i, c, p: (0, jnp.minimum(i, npt)),
                               memory_space=pltpu.SMEM),
                  pl.BlockSpec((MOE_TILE, d), lambda i, c, p: (jnp.minimum(i, npt - 1), 0)),
                  pl.BlockSpec((n_s, d), lambda i, c, p: (0, 0))],
        out_specs=pl.BlockSpec(memory_space=pl.ANY),
        scratch_shapes=[pltpu.VMEM((SUBLANES, d), F32), pltpu.SemaphoreType.DMA(())],
    )
    return pl.pallas_call(
        functools.partial(_moe_dispatch_kernel, n_prompt_tiles=npt),
        grid_spec=grid_spec,
        out_shape=jax.ShapeDtypeStruct((cap, d), F32),
        compiler_params=_params("arbitrary"),
        name="moe_dispatch",
    )(cnt, pstart, dest, u_prompt, u_sample)


def _moe_experts_kernel(be_ref, nu_ref, x_ref, wu_ref, bu_ref, wd_ref, bd_ref, y_ref):
    d_ff = wd_ref.shape[1]

    @pl.when(pl.program_id(0) < nu_ref[0])
    def _():
        x = x_ref[...].astype(BF16)
        hdn = jnp.dot(x, wu_ref[0], preferred_element_type=F32) + bu_ref[0]
        glu = jnp.minimum(hdn[:, :d_ff], SWIGLU_LIMIT)
        lin = jnp.clip(hdn[:, d_ff:], -SWIGLU_LIMIT, SWIGLU_LIMIT)
        act = glu * _sigmoid(SWIGLU_ALPHA * glu) * (lin + 1.0)
        y_ref[...] = jnp.dot(act.astype(BF16), wd_ref[0], preferred_element_type=F32) + bd_ref[0]

    @pl.when(pl.program_id(0) >= nu_ref[0])
    def _():
        y_ref[...] = jnp.zeros_like(y_ref)


def _moe_experts(xs, blk_exp, n_used, w_up, b_up, w_down, b_down):
    cap, d = xs.shape
    nb = cap // MOE_BLK
    last = lambda r, nu: jnp.minimum(r, nu[0] - 1)
    w_map = lambda r, be, nu: (be[last(r, nu)], 0, 0)
    x_map = lambda r, be, nu: (last(r, nu), 0)
    grid_spec = pltpu.PrefetchScalarGridSpec(
        num_scalar_prefetch=2,
        grid=(nb,),
        in_specs=[pl.BlockSpec((MOE_BLK, d), x_map),
                  pl.BlockSpec((1,) + w_up.shape[1:], w_map), pl.BlockSpec((1,) + b_up.shape[1:], w_map),
                  pl.BlockSpec((1,) + w_down.shape[1:], w_map), pl.BlockSpec((1,) + b_down.shape[1:], w_map)],
        out_specs=pl.BlockSpec((MOE_BLK, d), lambda r, be, nu: (r, 0)),
    )
    return pl.pallas_call(
        _moe_experts_kernel,
        grid_spec=grid_spec,
        out_shape=jax.ShapeDtypeStruct((cap, d), F32),
        compiler_params=_params("arbitrary"),
        name="moe_experts",
    )(blk_exp, n_used, xs, w_up, b_up, w_down, b_down)


def _combine_kernel(dest_ref, h_ref, p_ref, gate_ref, ys_ref, wg_ref, wp_ref, gf_ref, o_ref, y_scr, sem, *, final):
    tm = h_ref.shape[0]

    def start(t, carry):
        for j in range(TOP_K):
            _row_copy(ys_ref.at[pl.ds(dest_ref[j, t], 1)], y_scr.at[j, pl.ds(t, 1)], sem).start()
        return carry

    def wait(t, carry):
        for j in range(TOP_K):
            _row_copy(ys_ref.at[pl.ds(0, 1)], y_scr.at[0, pl.ds(0, 1)], sem).wait()
        return carry

    lax.fori_loop(0, tm, start, 0)
    lax.fori_loop(0, tm, wait, 0)
    gate = gate_ref[...]
    h = h_ref[...]
    for j in range(TOP_K):
        h = h + gate[:, j:j + 1] * y_scr[j]
    emb_gate = _sigmoid(jnp.dot(h.astype(BF16), wg_ref[...], preferred_element_type=F32))
    emb = jnp.dot(p_ref[...].astype(BF16), wp_ref[...], preferred_element_type=F32)
    h = h + emb_gate * emb
    if final:
        ms = jnp.mean(h * h, axis=-1, keepdims=True)
        h = h * lax.rsqrt(ms + NORM_EPS) * gf_ref[...]
    o_ref[...] = h


def _combine(h1, p_emb, dest, gate_t, ys, w_gate, w_emb, g_final, final):
    m, d = h1.shape
    tm = _row_tile(m, 256)
    full = lambda a: pl.BlockSpec(a.shape, lambda i: (0,) * a.ndim)
    return pl.pallas_call(
        functools.partial(_combine_kernel, final=final),
        grid=(m // tm,),
        in_specs=[pl.BlockSpec((TOP_K, tm), lambda i: (0, i), memory_space=pltpu.SMEM),
                  pl.BlockSpec((tm, d), lambda i: (i, 0)),
                  pl.BlockSpec((tm, p_emb.shape[1]), lambda i: (i, 0)),
                  pl.BlockSpec((tm, TOP_K), lambda i: (i, 0)),
                  pl.BlockSpec(memory_space=pl.ANY),
                  full(w_gate), full(w_emb), full(g_final)],
        out_specs=pl.BlockSpec((tm, d), lambda i: (i, 0)),
        out_shape=jax.ShapeDtypeStruct((m, d), F32),
        scratch_shapes=[pltpu.VMEM((TOP_K, tm, d), F32), pltpu.SemaphoreType.DMA(())],
        compiler_params=_params("arbitrary"),
        name="combine",
    )(dest, h1, p_emb, gate_t, ys, w_gate, w_emb, g_final)


def kernel(x_prompt, x_sample, state_hgrn, cache_k, cache_v, cache_logf, page_table, p_prompt, p_sample, g_mix, w_in, hg_lb_logits, g_hg_out, fox_f_bias, g_fox_out, w_out, g_mlp, w_router, b_router, w_up, b_up, w_down, b_down, w_ple, w_ple_gate, g_final):
    depth = w_in.shape[0]
    bsz, seq, d = x_prompt.shape
    dec_b = x_sample.shape[0]
    n_p, n_s = bsz * seq, dec_b
    n_phys = cache_k.shape[1]
    lb_all = jnp.cumsum(jax.nn.softmax(hg_lb_logits.astype(F32), axis=0), axis=0)
    hp = x_prompt.reshape(n_p, d)
    hs = x_sample.reshape(n_s, d)
    n_main = 7 * HG_WIDTH
    outs = [[] for _ in range(8)]
    for i in range(depth):
        final = i == depth - 1
        w_main = w_in[i, :, :n_main].astype(BF16)
        w_ff = jnp.pad(w_in[i, :, n_main:], ((0, 0), (0, LANES - FOX_HEADS))).astype(BF16)
        f_bias = jnp.pad(fox_f_bias[i], (0, LANES - FOX_HEADS)).reshape(1, LANES)
        g_mix_i = g_mix[i].reshape(1, d)
        lb_i = lb_all[i].reshape(1, HG_WIDTH)
        g_hg = g_hg_out[i].reshape(1, HG_WIDTH)
        g_fox = g_fox_out[i].reshape(1, FOX_WIDTH)
        w_out_i = w_out[i].astype(BF16)
        g_mlp_i = g_mlp[i].reshape(1, d)
        w_r_t = w_router[i].T
        b_r = b_router[i].reshape(N_EXPERTS, 1)

        q, f, v, gt, fq, fk, fv, kb, vb, lf, c = _inproj(hp, g_mix_i, w_main, w_ff, lb_i, f_bias, seq)
        oh, sp = _hgrn_prompt(q, f, v, gt, g_hg, bsz, seq)
        of = _fox_prompt(fq, kb, vb, c, c.T, g_fox, bsz, seq)
        h1_p, u_p, idx_p, gate_p = _outproj(oh, of, hp, w_out_i, g_mlp_i, w_r_t, b_r)
        outs[0].append(sp)
        outs[1].append(fk.reshape(bsz, seq, FOX_HEADS, FOX_DH))
        outs[2].append(fv.reshape(bsz, seq, FOX_HEADS, FOX_DH))
        outs[3].append(lf.reshape(bsz, seq, FOX_HEADS))

        q, f, v, gt, fq, fk, fv, kb, vb, lf, c = _inproj(hs, g_mix_i, w_main, w_ff, lb_i, f_bias, 1)
        oh, ss = _hgrn_step(q, f, v, gt, g_hg, state_hgrn[i])
        of = _fox_decode(fq, fk, fv, lf, g_fox,
                         cache_k[i].reshape(n_phys, PAGE_SIZE, FOX_WIDTH),
                         cache_v[i].reshape(n_phys, PAGE_SIZE, FOX_WIDTH),
                         cache_logf[i], page_table)
        h1_s, u_s, idx_s, gate_s = _outproj(oh, of, hs, w_out_i, g_mlp_i, w_r_t, b_r)
        outs[4].append(ss)
        outs[5].append(fk.reshape(dec_b, 1, FOX_HEADS, FOX_DH))
        outs[6].append(fv.reshape(dec_b, 1, FOX_HEADS, FOX_DH))
        outs[7].append(lf.reshape(dec_b, 1, FOX_HEADS))

        n_tile = -(-(n_p + n_s) // MOE_TILE) * MOE_TILE
        idx_all = jnp.concatenate(
            [idx_p, idx_s, jnp.full((TOP_K, n_tile - n_p - n_s), N_EXPERTS, jnp.int32)], axis=1)
        dest, blk_exp, n_used, cnt, pstart = _moe_plan(idx_all)
        xs = _moe_dispatch(u_p, u_s, dest, cnt, pstart)
        ys = _moe_experts(xs, blk_exp, n_used, w_up[i].astype(BF16), b_up[i].reshape(N_EXPERTS, 1, -1),
                          w_down[i].astype(BF16), b_down[i].reshape(N_EXPERTS, 1, -1))
        w_pg = w_ple_gate[i].astype(BF16)
        w_p = w_ple[i].astype(BF16)
        g_fin = g_final.reshape(1, d)
        hp = _combine(h1_p, p_prompt[i].reshape(n_p, -1), dest[:, :n_p], gate_p.T, ys, w_pg, w_p, g_fin, final)
        hs = _combine(h1_s, p_sample[i].reshape(n_s, -1), dest[:, n_p:n_p + n_s], gate_s.T, ys, w_pg, w_p, g_fin,
                      final)
    st = [jnp.stack(o) for o in outs]
    return (hp.reshape(bsz, seq, d), hs.reshape(dec_b, 1, d), *st)
```

```python
import functools

import jax
import jax.numpy as jnp
from jax import lax
from jax.experimental import pallas as pl
from jax.experimental.pallas import tpu as pltpu

F32 = jnp.float32
BF16 = jnp.bfloat16

HG_HEADS = 4
HG_DK = 128
HG_DV = 128
HG_WIDTH = HG_HEADS * HG_DK
FOX_HEADS = 8
FOX_DH = 64
FOX_WIDTH = FOX_HEADS * FOX_DH
N_EXPERTS = 32
TOP_K = 4
PAGE_SIZE = 128
NORM_EPS = 1e-6
SWIGLU_ALPHA = 1.702
SWIGLU_LIMIT = 7.0

LANES = 128
SUBLANES = 8
NEG_BIG = -1e30
VMEM_LIMIT = 56 * 1024 * 1024

HG_CHUNK = 16
HG_BLOCK = 128


def _params(*sem):
    return pltpu.CompilerParams(dimension_semantics=sem, vmem_limit_bytes=VMEM_LIMIT)


def _row_tile(m, pref):
    return pref if m % pref == 0 else m


def _split3(x):
    hi = x.astype(BF16)
    r1 = x - hi.astype(F32)
    mid = r1.astype(BF16)
    lo = (r1 - mid.astype(F32)).astype(BF16)
    return hi, mid, lo


def _dot01(mask_bf16, x):
    hi, mid, lo = _split3(x)
    d = lambda a: jnp.dot(mask_bf16, a, preferred_element_type=F32)
    return d(hi) + d(mid) + d(lo)


def _tri_lower(n, strict=False):
    r = lax.broadcasted_iota(jnp.int32, (n, n), 0)
    c = lax.broadcasted_iota(jnp.int32, (n, n), 1)
    m = (c < r) if strict else (c <= r)
    return jnp.where(m, 1.0, 0.0).astype(BF16)


def _sigmoid(x):
    return 1.0 / (1.0 + jnp.exp(-x))


def _silu(x):
    return x * _sigmoid(x)


def _log_sigmoid(x):
    return jnp.minimum(x, 0.0) - jnp.log1p(jnp.exp(-jnp.abs(x)))


def _inproj_kernel(x_ref, g_ref, w_ref, wf_ref, lb_ref, fb_ref,
                   q_ref, f_ref, i_ref, gt_ref, fq_ref, fk_ref, fv_ref, kb_ref, vb_ref, lf_ref, c_ref,
                   carry_ref, *, tiles_per_seq):
    x = x_ref[...]
    ms = jnp.mean(x * x, axis=-1, keepdims=True)
    u = (x * lax.rsqrt(ms + NORM_EPS) * g_ref[...]).astype(BF16)

    def proj(j):
        return jnp.dot(u, w_ref[:, j * HG_WIDTH:(j + 1) * HG_WIDTH], preferred_element_type=F32)

    q_ref[...] = _silu(proj(0))
    lb = lb_ref[...]
    f_ref[...] = lb + (1.0 - lb) * _sigmoid(proj(1))
    i_ref[...] = proj(2)
    gt_ref[...] = _silu(proj(3))
    fq_ref[...] = (proj(4) * (FOX_DH ** -0.5)).astype(BF16)
    fk = proj(5)
    fk_ref[...] = fk
    kb_ref[...] = fk.astype(BF16)
    fv = proj(6)
    fv_ref[...] = fv
    vb_ref[...] = fv.astype(BF16)
    ff = jnp.dot(u, wf_ref[...], preferred_element_type=F32) + fb_ref[...]
    lf = _log_sigmoid(ff)
    lf_ref[...] = lf[:, :FOX_HEADS]
    if tiles_per_seq == 0:
        c_ref[...] = lf[:, :FOX_HEADS]
    else:
        tm = x.shape[0]

        @pl.when(pl.program_id(0) % tiles_per_seq == 0)
        def _():
            carry_ref[...] = jnp.zeros_like(carry_ref)

        c = _dot01(_tri_lower(tm), lf) + carry_ref[0:1, :]
        carry_ref[...] = jnp.broadcast_to(c[tm - 1:tm, :], carry_ref.shape)
        c_ref[...] = c[:, :FOX_HEADS]


def _inproj(x2d, g_mix, w_main, w_ff, lb, f_bias, seq_len):
    m, d = x2d.shape
    tm = _row_tile(m if seq_len == 1 else seq_len, 512)
    tiles_per_seq = 0 if seq_len == 1 else seq_len // tm
    row = lambda w: pl.BlockSpec((tm, w), lambda i: (i, 0))
    full = lambda a: pl.BlockSpec(a.shape, lambda i: (0,) * a.ndim)
    wide = jax.ShapeDtypeStruct((m, HG_WIDTH), F32)
    wide_bf = jax.ShapeDtypeStruct((m, HG_WIDTH), BF16)
    narrow = jax.ShapeDtypeStruct((m, FOX_HEADS), F32)
    return pl.pallas_call(
        functools.partial(_inproj_kernel, tiles_per_seq=tiles_per_seq),
        grid=(m // tm,),
        in_specs=[row(d), full(g_mix), full(w_main), full(w_ff), full(lb), full(f_bias)],
        out_specs=[row(HG_WIDTH)] * 9 + [row(FOX_HEADS)] * 2,
        out_shape=[wide, wide, wide, wide, wide_bf, wide, wide, wide_bf, wide_bf, narrow, narrow],
        scratch_shapes=[pltpu.VMEM((SUBLANES, LANES), F32)],
        compiler_params=_params("arbitrary"),
        name="inproj",
    )(x2d, g_mix, w_main, w_ff, lb, f_bias)


def _hgrn_kernel(q_ref, f_ref, v_ref, gt_ref, gn_ref, o_ref, s_ref, s_scr, b_scr):
    tb = q_ref.shape[0]

    @pl.when(pl.program_id(1) == 0)
    def _():
        s_scr[...] = jnp.zeros_like(s_scr)

    tri = _tri_lower(HG_BLOCK)
    row_id = lax.broadcasted_iota(jnp.int32, (HG_CHUNK, 1), 0)

    def block_body(blk, carry):
        r_blk = pl.multiple_of(blk * HG_BLOCK, HG_BLOCK)
        b_scr[...] = _dot01(tri, jnp.log(f_ref[pl.ds(r_blk, HG_BLOCK), :]))

        def chunk_body(ci, carry2):
            r_loc = pl.multiple_of(ci * HG_CHUNK, HG_CHUNK)
            r0 = r_blk + r_loc
            for h in range(HG_HEADS):
                cols = slice(h * HG_DK, (h + 1) * HG_DK)
                q = q_ref[pl.ds(r0, HG_CHUNK), cols]
                k = 1.0 - f_ref[pl.ds(r0, HG_CHUNK), cols]
                v = v_ref[pl.ds(r0, HG_CHUNK), cols]
                bb = b_scr[pl.ds(r_loc, HG_CHUNK), cols]
                b_prev = bb[0:1, :] - jnp.log(f_ref[pl.ds(r0, 1), cols])
                b = bb - b_prev
                s_old = s_scr[h]
                o = jnp.dot((q * jnp.exp(b)).astype(BF16), s_old.astype(BF16), preferred_element_type=F32)
                for s in range(HG_CHUNK):
                    d = jnp.where(row_id >= s, b - b[s:s + 1, :], NEG_BIG)
                    w = jnp.exp(d) * q * k[s:s + 1, :]
                    o = o + jnp.sum(w, axis=-1, keepdims=True) * v[s:s + 1, :]
                b_last = b[HG_CHUNK - 1:HG_CHUNK, :]
                kd = (k * jnp.exp(b_last - b)).astype(BF16)
                upd = lax.dot_general(kd, v.astype(BF16), (((0,), (0,)), ((), ())),
                                      preferred_element_type=F32)
                eb = jnp.transpose(jnp.broadcast_to(jnp.exp(b_last), (HG_DK, HG_DK)))
                s_scr[h] = eb * s_old + upd
                ms = jnp.mean(o * o, axis=-1, keepdims=True)
                o = o * lax.rsqrt(ms + NORM_EPS) * gn_ref[:, cols] * gt_ref[pl.ds(r0, HG_CHUNK), cols]
                o_ref[pl.ds(r0, HG_CHUNK), cols] = o.astype(BF16)
            return carry2

        return lax.fori_loop(0, HG_BLOCK // HG_CHUNK, chunk_body, carry)

    lax.fori_loop(0, tb // HG_BLOCK, block_body, 0)

    @pl.when(pl.program_id(1) == pl.num_programs(1) - 1)
    def _():
        s_ref[0] = s_scr[...]


def _hgrn_prompt(q, f, v, gt, g_norm, batch, seq):
    tb = _row_tile(seq, 512)
    assert tb % HG_BLOCK == 0
    nt = seq // tb
    row = pl.BlockSpec((tb, HG_WIDTH), lambda b, t: (b * nt + t, 0))
    return pl.pallas_call(
        _hgrn_kernel,
        grid=(batch, nt),
        in_specs=[row, row, row, row, pl.BlockSpec((1, HG_WIDTH), lambda b, t: (0, 0))],
        out_specs=[row, pl.BlockSpec((1, HG_HEADS, HG_DK, HG_DV), lambda b, t: (b, 0, 0, 0))],
        out_shape=[jax.ShapeDtypeStruct((batch * seq, HG_WIDTH), BF16),
                   jax.ShapeDtypeStruct((batch, HG_HEADS, HG_DK, HG_DV), F32)],
        scratch_shapes=[pltpu.VMEM((HG_HEADS, HG_DK, HG_DV), F32), pltpu.VMEM((HG_BLOCK, HG_WIDTH), F32)],
        compiler_params=_params("arbitrary", "arbitrary"),
        name="hgrn_prompt",
    )(q, f, v, gt, g_norm)


def _fox_prompt_kernel(q_ref, k_ref, v_ref, cq_ref, ck_ref, gn_ref, o_ref, m_scr, l_scr, acc_scr):
    pair = pl.program_id(1)
    qi = pl.program_id(2)
    ki = pl.program_id(3)
    tq = q_ref.shape[0]
    tk = k_ref.shape[0]

    @pl.when(ki == 0)
    def _():
        m_scr[...] = jnp.full_like(m_scr, NEG_BIG)
        l_scr[...] = jnp.zeros_like(l_scr)
        acc_scr[...] = jnp.zeros_like(acc_scr)

    @pl.when(ki <= qi)
    def _():
        lane = lax.broadcasted_iota(jnp.int32, (1, LANES), 1)
        first = lane < FOX_DH
        q = q_ref[...]
        k = k_ref[...]
        v = v_ref[...]
        q_pos = qi * tq + lax.broadcasted_iota(jnp.int32, (tq, tk), 0)
        k_pos = ki * tk + lax.broadcasted_iota(jnp.int32, (tq, tk), 1)
        causal = k_pos <= q_pos
        cq_all = cq_ref[...]
        ck_all = ck_ref[...]
        head_col = lax.broadcasted_iota(jnp.int32, (1, FOX_HEADS), 1)
        head_row = lax.broadcasted_iota(jnp.int32, (FOX_HEADS, 1), 0)
        pv = []
        for j in range(2):
            h = 2 * pair + j
            qh = jnp.where(first if j == 0 else ~first, q, jnp.zeros_like(q))
            s = lax.dot_general(qh, k, (((1,), (1,)), ((), ())), preferred_element_type=F32)
            cq = jnp.sum(jnp.where(head_col == h, cq_all, 0.0), axis=1, keepdims=True)
            ck = jnp.sum(jnp.where(head_row == h, ck_all, 0.0), axis=0, keepdims=True)
            s = jnp.where(causal, s + cq - ck, NEG_BIG)
            m_old = m_scr[j]
            m_new = jnp.maximum(m_old, jnp.max(s, axis=-1, keepdims=True))
            alpha = jnp.exp(m_old - m_new)
            p = jnp.exp(s - m_new)
            l_scr[j] = alpha * l_scr[j] + jnp.sum(p, axis=-1, keepdims=True)
            m_scr[j] = m_new
            pv.append((alpha, jnp.dot(p.astype(BF16), v, preferred_element_type=F32)))
        alpha = jnp.where(first, pv[0][0], pv[1][0])
        acc_scr[...] = alpha * acc_scr[...] + jnp.where(first, pv[0][1], pv[1][1])

    @pl.when(ki == qi)
    def _():
        lane = lax.broadcasted_iota(jnp.int32, (1, LANES), 1)
        first = lane < FOX_DH
        o = acc_scr[...] / jnp.where(first, l_scr[0], l_scr[1])
        sq = o * o
        ss0 = jnp.sum(jnp.where(first, sq, 0.0), axis=-1, keepdims=True)
        ss1 = jnp.sum(jnp.where(first, 0.0, sq), axis=-1, keepdims=True)
        ms = jnp.where(first, ss0, ss1) * (1.0 / FOX_DH)
        o_ref[...] = (o * lax.rsqrt(ms + NORM_EPS) * gn_ref[...]).astype(BF16)


def _fox_prompt(q_bf, k_bf, v_bf, c_col, c_row, g_norm, batch, seq):
    tq = _row_tile(seq, 512)
    nq = seq // tq
    pairs = FOX_HEADS // 2
    kv_map = lambda b, p, i, j: (b * nq + jnp.minimum(j, i), p)
    return pl.pallas_call(
        _fox_prompt_kernel,
        grid=(batch, pairs, nq, nq),
        in_specs=[pl.BlockSpec((tq, LANES), lambda b, p, i, j: (b * nq + i, p)),
                  pl.BlockSpec((tq, LANES), kv_map),
                  pl.BlockSpec((tq, LANES), kv_map),
                  pl.BlockSpec((tq, FOX_HEADS), lambda b, p, i, j: (b * nq + i, 0)),
                  pl.BlockSpec((FOX_HEADS, tq), lambda b, p, i, j: (0, b * nq + jnp.minimum(j, i))),
                  pl.BlockSpec((1, LANES), lambda b, p, i, j: (0, p))],
        out_specs=pl.BlockSpec((tq, LANES), lambda b, p, i, j: (b * nq + i, p)),
        out_shape=jax.ShapeDtypeStruct((batch * seq, FOX_WIDTH), BF16),
        scratch_shapes=[pltpu.VMEM((2, tq, 1), F32), pltpu.VMEM((2, tq, 1), F32), pltpu.VMEM((tq, LANES), F32)],
        compiler_params=_params("arbitrary", "arbitrary", "arbitrary", "arbitrary"),
        name="fox_prompt",
    )(q_bf, k_bf, v_bf, c_col, c_row, g_norm)


def _hgrn_step_kernel(q_ref, f_ref, v_ref, gt_ref, gn_ref, s_ref, o_ref, so_ref):
    rows = q_ref.shape[0]

    def col(row):
        return jnp.transpose(jnp.broadcast_to(row, (HG_DK, HG_DK)))

    for r in range(rows):
        for h in range(HG_HEADS):
            cols = slice(h * HG_DK, (h + 1) * HG_DK)
            f_col = col(f_ref[r:r + 1, cols])
            q_col = col(q_ref[r:r + 1, cols])
            s_new = f_col * s_ref[r, h] + (1.0 - f_col) * v_ref[r:r + 1, cols]
            so_ref[r, h] = s_new
            o = jnp.sum(s_new * q_col, axis=0, keepdims=True)
            ms = jnp.mean(o * o, axis=-1, keepdims=True)
            o = o * lax.rsqrt(ms + NORM_EPS) * gn_ref[:, cols] * gt_ref[r:r + 1, cols]
            o_ref[r:r + 1, cols] = o.astype(BF16)


def _hgrn_step(q, f, v, gt, g_norm, state):
    n = q.shape[0]
    rows = SUBLANES
    assert n % rows == 0
    row = pl.BlockSpec((rows, HG_WIDTH), lambda i: (i, 0))
    st = pl.BlockSpec((rows, HG_HEADS, HG_DK, HG_DV), lambda i: (i, 0, 0, 0))
    return pl.pallas_call(
        _hgrn_step_kernel,
        grid=(n // rows,),
        in_specs=[row, row, row, row, pl.BlockSpec((1, HG_WIDTH), lambda i: (0, 0)), st],
        out_specs=[row, st],
        out_shape=[jax.ShapeDtypeStruct((n, HG_WIDTH), BF16), jax.ShapeDtypeStruct(state.shape, F32)],
        compiler_params=_params("arbitrary"),
        name="hgrn_step",
    )(q, f, v, gt, g_norm, state)


FOX_PAGES_PER_STEP = 16


def _dot01_rhs(x, mask_bf16):
    hi, mid, lo = _split3(x)
    d = lambda a: jnp.dot(a, mask_bf16, preferred_element_type=F32)
    return d(hi) + d(mid) + d(lo)


def _fox_decode_kernel(pt_ref, q_ref, kn_ref, vn_ref, lfn_ref, gn_ref, *refs):
    del pt_ref
    npg = FOX_PAGES_PER_STEP
    k_refs, v_refs, lf_refs = refs[:npg], refs[npg:2 * npg], refs[2 * npg:3 * npg]
    o_ref, m_scr, l_scr, r_scr, acc_scr = refs[3 * npg:]
    c = pl.program_id(1)
    head = lax.broadcasted_iota(jnp.int32, (FOX_HEADS, FOX_WIDTH), 0)
    lane = lax.broadcasted_iota(jnp.int32, (FOX_HEADS, FOX_WIDTH), 1)
    own = lane // FOX_DH == head
    q_rows = jnp.where(own, q_ref[0].astype(F32), 0.0)
    lf_new = lfn_ref[0]

    @pl.when(c == 0)
    def _():
        k_new = kn_ref[0].astype(BF16).astype(F32)
        m_scr[...] = jnp.sum(q_rows * k_new, axis=1, keepdims=True)
        l_scr[...] = jnp.ones_like(l_scr)
        r_scr[...] = jnp.zeros_like(r_scr)
        acc_scr[...] = jnp.broadcast_to(vn_ref[0], acc_scr.shape)

    q_bf = q_rows.astype(BF16)
    rr = lax.broadcasted_iota(jnp.int32, (PAGE_SIZE, PAGE_SIZE), 0)
    cc = lax.broadcasted_iota(jnp.int32, (PAGE_SIZE, PAGE_SIZE), 1)
    later = jnp.where(rr > cc, 1.0, 0.0).astype(BF16)

    run = r_scr[...]
    scores = []
    for g in range(npg):
        s = jnp.dot(q_bf, k_refs[g][0].astype(BF16), preferred_element_type=F32)
        lf = lf_refs[g][0]
        scores.append(s + _dot01_rhs(lf, later) + run + lf_new)
        run = run + jnp.sum(lf, axis=1, keepdims=True)
    r_scr[...] = run

    m_old = m_scr[...]
    m_new = m_old
    for s in scores:
        m_new = jnp.maximum(m_new, jnp.max(s, axis=1, keepdims=True))
    alpha = jnp.exp(m_old - m_new)
    l_new = alpha * l_scr[...]
    pv = jnp.zeros(acc_scr.shape, F32)
    for g in range(npg):
        p = jnp.exp(scores[g] - m_new)
        l_new = l_new + jnp.sum(p, axis=1, keepdims=True)
        pv = pv + lax.dot_general(p.astype(BF16), v_refs[g][0].astype(BF16), (((1,), (1,)), ((), ())),
                                  preferred_element_type=F32)
    m_scr[...] = m_new
    l_scr[...] = l_new
    acc_scr[...] = alpha * acc_scr[...] + pv

    @pl.when(c == pl.num_programs(1) - 1)
    def _():
        o = jnp.where(own, acc_scr[...] / l_scr[...], 0.0)
        ms = jnp.sum(o * o, axis=1, keepdims=True) * (1.0 / FOX_DH)
        o = o * lax.rsqrt(ms + NORM_EPS)
        o_ref[0] = (jnp.sum(o, axis=0, keepdims=True) * gn_ref[...]).astype(BF16)


def _fox_decode(q_bf, k_new, v_new, lf_new, g_norm, cache_kt, cache_vt, cache_lft, page_table):
    n, n_pages = page_table.shape
    npg = FOX_PAGES_PER_STEP
    assert n_pages % npg == 0
    n_chunks = n_pages // npg

    def page_map(g):
        return lambda b, c, pt: (pt[b, (n_chunks - 1 - c) * npg + (npg - 1 - g)], 0, 0)

    row3 = lambda h, w: pl.BlockSpec((1, h, w), lambda b, c, pt: (b, 0, 0))
    kv_specs = [pl.BlockSpec((1, FOX_WIDTH, PAGE_SIZE), page_map(g)) for g in range(npg)]
    lf_specs = [pl.BlockSpec((1, FOX_HEADS, PAGE_SIZE), page_map(g)) for g in range(npg)]
    stat = pltpu.VMEM((FOX_HEADS, 1), F32)
    grid_spec = pltpu.PrefetchScalarGridSpec(
        num_scalar_prefetch=1,
        grid=(n, n_chunks),
        in_specs=[row3(1, FOX_WIDTH), row3(1, FOX_WIDTH), row3(1, FOX_WIDTH), row3(FOX_HEADS, 1),
                  pl.BlockSpec((1, FOX_WIDTH), lambda b, c, pt: (0, 0))] + kv_specs + kv_specs + lf_specs,
        out_specs=row3(1, FOX_WIDTH),
        scratch_shapes=[stat, stat, stat, pltpu.VMEM((FOX_HEADS, FOX_WIDTH), F32)],
    )
    out = pl.pallas_call(
        _fox_decode_kernel,
        grid_spec=grid_spec,
        out_shape=jax.ShapeDtypeStruct((n, 1, FOX_WIDTH), BF16),
        compiler_params=_params("arbitrary", "arbitrary"),
        name="fox_decode",
    )(page_table, q_bf.reshape(n, 1, FOX_WIDTH), k_new.reshape(n, 1, FOX_WIDTH), v_new.reshape(n, 1, FOX_WIDTH),
      lf_new.reshape(n, FOX_HEADS, 1), g_norm, *([cache_kt] * npg), *([cache_vt] * npg), *([cache_lft] * npg))
    return out.reshape(n, FOX_WIDTH)


def _outproj_kernel(oh_ref, of_ref, x_ref, w_ref, g_ref, wr_ref, br_ref, h_ref, u_ref, idx_ref, gate_ref):
    mix_in = jnp.concatenate([oh_ref[...], of_ref[...]], axis=-1)
    h = x_ref[...] + jnp.dot(mix_in, w_ref[...], preferred_element_type=F32)
    h_ref[...] = h
    ms = jnp.mean(h * h, axis=-1, keepdims=True)
    u = h * lax.rsqrt(ms + NORM_EPS) * g_ref[...]
    u_ref[...] = u
    vals = lax.dot_general(wr_ref[...], u, (((1,), (1,)), ((), ())), preferred_element_type=F32,
                           precision=lax.Precision.HIGHEST) + br_ref[...]
    expert = lax.broadcasted_iota(jnp.int32, vals.shape, 0)
    top_v, top_i = [], []
    for _ in range(TOP_K):
        m = jnp.max(vals, axis=0, keepdims=True)
        sel = jnp.min(jnp.where(vals == m, expert, N_EXPERTS), axis=0, keepdims=True)
        vals = jnp.where(expert == sel, -jnp.inf, vals)
        top_v.append(m)
        top_i.append(sel)
    ex = [jnp.exp(t - top_v[0]) for t in top_v]
    den = ex[0] + ex[1] + ex[2] + ex[3]
    idx_ref[...] = jnp.concatenate(top_i, axis=0)
    gate_ref[...] = jnp.concatenate([x / den for x in ex], axis=0)


def _outproj(oh, of, x, w_out, g_mlp, w_router_t, b_router):
    m, d = x.shape
    tm = _row_tile(m, 512)
    row = lambda w: pl.BlockSpec((tm, w), lambda i: (i, 0))
    full = lambda a: pl.BlockSpec(a.shape, lambda i: (0,) * a.ndim)
    colblk = pl.BlockSpec((TOP_K, tm), lambda i: (0, i))
    return pl.pallas_call(
        _outproj_kernel,
        grid=(m // tm,),
        in_specs=[row(HG_WIDTH), row(FOX_WIDTH), row(d), full(w_out), full(g_mlp), full(w_router_t),
                  full(b_router)],
        out_specs=[row(d), row(d), colblk, colblk],
        out_shape=[jax.ShapeDtypeStruct((m, d), F32), jax.ShapeDtypeStruct((m, d), F32),
                   jax.ShapeDtypeStruct((TOP_K, m), jnp.int32), jax.ShapeDtypeStruct((TOP_K, m), F32)],
        compiler_params=_params("arbitrary"),
        name="outproj",
    )(oh, of, x, w_out, g_mlp, w_router_t, b_router)


MOE_TILE = 512
MOE_BLK = 256


def _moe_plan_kernel(idx_ref, dest_ref, be_ref, nu_ref, cnt_ref, ps_ref, cnt_scr, run_scr, ps_scr):
    phase = pl.program_id(0)
    step = pl.program_id(1)
    tm = idx_ref.shape[1]
    expert = lax.broadcasted_iota(jnp.int32, (N_EXPERTS, tm), 0)
    idx = idx_ref[...]
    onehot = [expert == idx[j:j + 1, :] for j in range(TOP_K)]

    @pl.when((phase == 0) & (step == 0))
    def _():
        cnt_scr[...] = jnp.zeros_like(cnt_scr)

    @pl.when(phase == 0)
    def _():
        tot = jnp.zeros((N_EXPERTS, 1), F32)
        for oh in onehot:
            tot = tot + jnp.sum(jnp.where(oh, 1.0, 0.0), axis=1, keepdims=True)
        cnt_scr[...] = cnt_scr[...] + tot

    @pl.when((phase == 1) & (step == 0))
    def _():
        cnt = cnt_scr[...]
        padded = jnp.floor((cnt + (MOE_BLK - 1)) * (1.0 / MOE_BLK)) * MOE_BLK
        pstart = _dot01(_tri_lower(N_EXPERTS, strict=True), padded)
        pend = pstart + padded
        ps_scr[...] = pstart
        run_scr[...] = jnp.zeros_like(run_scr)
        cnt_ref[...] = cnt.astype(jnp.int32)
        ps_ref[...] = pstart.astype(jnp.int32)
        nb = be_ref.shape[1]
        row0 = lax.broadcasted_iota(jnp.int32, (1, nb), 1).astype(F32) * MOE_BLK
        be = jnp.sum(jnp.where(pend[:, 0:1] <= row0, 1.0, 0.0), axis=0, keepdims=True)
        be_ref[...] = jnp.minimum(be, N_EXPERTS - 1).astype(jnp.int32)
        total = jnp.max(pend[:, 0:1], axis=0, keepdims=True)
        nu_ref[...] = jnp.broadcast_to(total * (1.0 / MOE_BLK), nu_ref.shape).astype(jnp.int32)

    @pl.when(phase == 1)
    def _():
        r = lax.broadcasted_iota(jnp.int32, (tm, tm), 0)
        c = lax.broadcasted_iota(jnp.int32, (tm, tm), 1)
        before = jnp.where(r < c, 1.0, 0.0).astype(BF16)
        run = run_scr[:, 0:1]
        base = ps_scr[:, 0:1]
        rows = []
        for oh in onehot:
            ohf = jnp.where(oh, 1.0, 0.0)
            prefix = jnp.dot(ohf.astype(BF16), before, preferred_element_type=F32)
            rows.append(jnp.sum(ohf * (base + run + prefix), axis=0, keepdims=True))
            run = run + jnp.sum(ohf, axis=1, keepdims=True)
        run_scr[...] = jnp.broadcast_to(run, run_scr.shape)
        dest_ref[...] = jnp.concatenate(rows, axis=0).astype(jnp.int32)


def _moe_blocks(n_tok):
    return n_tok * TOP_K // MOE_BLK + N_EXPERTS


def _moe_plan(idx_all):
    n_tok = idx_all.shape[1]
    nb = -(-_moe_blocks(n_tok) // LANES) * LANES
    lane_i32 = jax.ShapeDtypeStruct((N_EXPERTS, LANES), jnp.int32)
    const = lambda shape: pl.BlockSpec(shape, lambda p, s: (0, 0))
    dest, be, nu, cnt, ps = pl.pallas_call(
        _moe_plan_kernel,
        grid=(2, n_tok // MOE_TILE),
        in_specs=[pl.BlockSpec((TOP_K, MOE_TILE), lambda p, s: (0, s))],
        out_specs=[pl.BlockSpec((TOP_K, MOE_TILE), lambda p, s: (0, s * p)),
                   const((1, nb)), const((1, LANES)), const((N_EXPERTS, LANES)), const((N_EXPERTS, LANES))],
        out_shape=[jax.ShapeDtypeStruct((TOP_K, n_tok), jnp.int32), jax.ShapeDtypeStruct((1, nb), jnp.int32),
                   jax.ShapeDtypeStruct((1, LANES), jnp.int32), lane_i32, lane_i32],
        scratch_shapes=[pltpu.VMEM((N_EXPERTS, LANES), F32)] * 3,
        compiler_params=_params("arbitrary", "arbitrary"),
        name="moe_plan",
    )(idx_all)
    return dest, be[0], nu[0, :1], cnt[:, 0], ps[:, 0]


def _row_copy(src, dst, sem):
    return pltpu.make_async_copy(src, dst, sem)


def _moe_dispatch_kernel(cnt_ref, ps_ref, dest_ref, up_ref, us_ref, xs_ref, zero_scr, sem, *, n_prompt_tiles):
    i = pl.program_id(0)

    def scatter(src_ref):
        n = src_ref.shape[0]

        def start(t, carry):
            for j in range(TOP_K):
                _row_copy(src_ref.at[pl.ds(t, 1)], xs_ref.at[pl.ds(dest_ref[j, t], 1)], sem).start(priority=j % 2)
            return carry

        def wait(t, carry):
            for j in range(TOP_K):
                _row_copy(src_ref.at[pl.ds(0, 1)], xs_ref.at[pl.ds(0, 1)], sem).wait()
            return carry

        lax.fori_loop(0, n, start, 0)
        lax.fori_loop(0, n, wait, 0)

    @pl.when(i < n_prompt_tiles)
    def _():
        scatter(up_ref)

    @pl.when(i == n_prompt_tiles)
    def _():
        scatter(us_ref)

    @pl.when(i == n_prompt_tiles + 1)
    def _():
        zero_scr[...] = jnp.zeros_like(zero_scr)

        def fill(e, start_not_wait):
            cnt = cnt_ref[e]
            pad = (-cnt) & (MOE_BLK - 1)
            off = ps_ref[e] + cnt

            def one(r, carry):
                cp = _row_copy(zero_scr.at[pl.ds(0, 1)], xs_ref.at[pl.ds(off + r, 1)], sem)
                if start_not_wait:
                    cp.start()
                else:
                    cp.wait()
                return carry

            lax.fori_loop(0, pad, one, 0)

        lax.fori_loop(0, N_EXPERTS, lambda e, c: (fill(e, True), c)[1], 0)
        lax.fori_loop(0, N_EXPERTS, lambda e, c: (fill(e, False), c)[1], 0)

        last = N_EXPERTS - 1
        used = (ps_ref[last] + cnt_ref[last] + (MOE_BLK - 1)) // MOE_BLK
        n_blocks = xs_ref.shape[0] // MOE_BLK

        def tail(start_not_wait):
            def one(b, carry):
                row0 = pl.multiple_of(b * MOE_BLK, MOE_BLK)
                for r in range(0, MOE_BLK, SUBLANES):
                    cp = _row_copy(zero_scr, xs_ref.at[pl.ds(row0 + r, SUBLANES)], sem)
                    if start_not_wait:
                        cp.start()
                    else:
                        cp.wait()
                return carry

            lax.fori_loop(used, n_blocks, one, 0)

        tail(True)
        tail(False)


def _moe_dispatch(u_prompt, u_sample, dest, cnt, pstart):
    n_p, d = u_prompt.shape
    n_s = u_sample.shape[0]
    n_tok = dest.shape[1]
    assert n_p % MOE_TILE == 0 and n_s <= MOE_TILE and n_tok == n_p + MOE_TILE
    npt = n_p // MOE_TILE
    cap = _moe_blocks(n_tok) * MOE_BLK
    grid_spec = pltpu.PrefetchScalarGridSpec(
        num_scalar_prefetch=2,
        grid=(npt + 2,),
        in_specs=[pl.BlockSpec((TOP_K, MOE_TILE), lambda i, c, p: (0, jnp.minimum(i, npt)),
                               memory_space=pltpu.SMEM),
                  pl.BlockSpec((MOE_TILE, d), lambda i, c, p: (jnp.minimum(i, npt - 1), 0)),
                  pl.BlockSpec((n_s, d), lambda i, c, p: (0, 0))],
        out_specs=pl.BlockSpec(memory_space=pl.ANY),
        scratch_shapes=[pltpu.VMEM((SUBLANES, d), F32), pltpu.SemaphoreType.DMA(())],
    )
    return pl.pallas_call(
        functools.partial(_moe_dispatch_kernel, n_prompt_tiles=npt),
        grid_spec=grid_spec,
        out_shape=jax.ShapeDtypeStruct((cap, d), F32),
        compiler_params=_params("arbitrary"),
        name="moe_dispatch",
    )(cnt, pstart, dest, u_prompt, u_sample)


def _moe_experts_kernel(be_ref, nu_ref, x_ref, wu_ref, bu_ref, wd_ref, bd_ref, y_ref, wu_bf, wd_bf):
    d_ff = wd_ref.shape[1]
    r = pl.program_id(0)
    active = r < nu_ref[0]
    new_expert = (r == 0) | (be_ref[r] != be_ref[jnp.maximum(r - 1, 0)])

    @pl.when(active & new_expert)
    def _():
        wu_bf[...] = wu_ref[0].astype(BF16)
        wd_bf[...] = wd_ref[0].astype(BF16)

    @pl.when(active)
    def _():
        x = x_ref[...].astype(BF16)
        hdn = jnp.dot(x, wu_bf[...], preferred_element_type=F32) + bu_ref[0]
        glu = jnp.minimum(hdn[:, :d_ff], SWIGLU_LIMIT)
        lin = jnp.clip(hdn[:, d_ff:], -SWIGLU_LIMIT, SWIGLU_LIMIT)
        act = glu * _sigmoid(SWIGLU_ALPHA * glu) * (lin + 1.0)
        y_ref[...] = jnp.dot(act.astype(BF16), wd_bf[...], preferred_element_type=F32) + bd_ref[0]

    @pl.when(pl.program_id(0) >= nu_ref[0])
    def _():
        y_ref[...] = jnp.zeros_like(y_ref)


def _moe_experts(xs, blk_exp, n_used, w_up, b_up, w_down, b_down):
    cap, d = xs.shape
    nb = cap // MOE_BLK
    last = lambda r, nu: jnp.minimum(r, nu[0] - 1)
    w_map = lambda r, be, nu: (be[last(r, nu)], 0, 0)
    x_map = lambda r, be, nu: (last(r, nu), 0)
    grid_spec = pltpu.PrefetchScalarGridSpec(
        num_scalar_prefetch=2,
        grid=(nb,),
        in_specs=[pl.BlockSpec((MOE_BLK, d), x_map),
                  pl.BlockSpec((1,) + w_up.shape[1:], w_map), pl.BlockSpec((1,) + b_up.shape[1:], w_map),
                  pl.BlockSpec((1,) + w_down.shape[1:], w_map), pl.BlockSpec((1,) + b_down.shape[1:], w_map)],
        out_specs=pl.BlockSpec((MOE_BLK, d), lambda r, be, nu: (r, 0)),
        scratch_shapes=[pltpu.VMEM(w_up.shape[1:], BF16), pltpu.VMEM(w_down.shape[1:], BF16)],
    )
    return pl.pallas_call(
        _moe_experts_kernel,
        grid_spec=grid_spec,
        out_shape=jax.ShapeDtypeStruct((cap, d), F32),
        compiler_params=_params("arbitrary"),
        name="moe_experts",
    )(blk_exp, n_used, xs, w_up, b_up, w_down, b_down)


def _combine_kernel(dest_ref, dest_next_ref, h_ref, p_ref, gate_ref, ys_ref, wg_ref, wp_ref, gf_ref, o_ref,
                    y_scr, sem, *, final):
    tm = h_ref.shape[0]
    i = pl.program_id(0)
    slot = i % 2

    def gather(idx_ref, s):
        def start(t, carry):
            for j in range(TOP_K):
                _row_copy(ys_ref.at[pl.ds(idx_ref[j, t], 1)], y_scr.at[s, j, pl.ds(t, 1)],
                          sem.at[s]).start(priority=j % 2)
            return carry

        lax.fori_loop(0, tm, start, 0)

    @pl.when(i == 0)
    def _():
        gather(dest_ref, slot)

    @pl.when(i + 1 < pl.num_programs(0))
    def _():
        gather(dest_next_ref, 1 - slot)

    def wait(t, carry):
        for j in range(TOP_K):
            _row_copy(ys_ref.at[pl.ds(0, 1)], y_scr.at[slot, 0, pl.ds(0, 1)], sem.at[slot]).wait()
        return carry

    lax.fori_loop(0, tm, wait, 0)
    gate = gate_ref[...]
    h = h_ref[...]
    for j in range(TOP_K):
        h = h + gate[:, j:j + 1] * y_scr[slot, j]
    emb_gate = _sigmoid(jnp.dot(h.astype(BF16), wg_ref[...], preferred_element_type=F32))
    emb = jnp.dot(p_ref[...].astype(BF16), wp_ref[...], preferred_element_type=F32)
    h = h + emb_gate * emb
    if final:
        ms = jnp.mean(h * h, axis=-1, keepdims=True)
        h = h * lax.rsqrt(ms + NORM_EPS) * gf_ref[...]
    o_ref[...] = h


def _combine(h1, p_emb, dest, gate_t, ys, w_gate, w_emb, g_final, final):
    m, d = h1.shape
    tm = _row_tile(m, 256)
    full = lambda a: pl.BlockSpec(a.shape, lambda i: (0,) * a.ndim)
    nt = m // tm
    return pl.pallas_call(
        functools.partial(_combine_kernel, final=final),
        grid=(nt,),
        in_specs=[pl.BlockSpec((TOP_K, tm), lambda i: (0, i), memory_space=pltpu.SMEM),
                  pl.BlockSpec((TOP_K, tm), lambda i: (0, jnp.minimum(i + 1, nt - 1)), memory_space=pltpu.SMEM),
                  pl.BlockSpec((tm, d), lambda i: (i, 0)),
                  pl.BlockSpec((tm, p_emb.shape[1]), lambda i: (i, 0)),
                  pl.BlockSpec((tm, TOP_K), lambda i: (i, 0)),
                  pl.BlockSpec(memory_space=pl.ANY),
                  full(w_gate), full(w_emb), full(g_final)],
        out_specs=pl.BlockSpec((tm, d), lambda i: (i, 0)),
        out_shape=jax.ShapeDtypeStruct((m, d), F32),
        scratch_shapes=[pltpu.VMEM((2, TOP_K, tm, d), F32), pltpu.SemaphoreType.DMA((2,))],
        compiler_params=_params("arbitrary"),
        name="combine",
    )(dest, dest, h1, p_emb, gate_t, ys, w_gate, w_emb, g_final)


def kernel(x_prompt, x_sample, state_hgrn, cache_k, cache_v, cache_logf, page_table, p_prompt, p_sample, g_mix, w_in, hg_lb_logits, g_hg_out, fox_f_bias, g_fox_out, w_out, g_mlp, w_router, b_router, w_up, b_up, w_down, b_down, w_ple, w_ple_gate, g_final):
    depth = w_in.shape[0]
    bsz, seq, d = x_prompt.shape
    dec_b = x_sample.shape[0]
    n_p, n_s = bsz * seq, dec_b
    n_phys = cache_k.shape[1]
    lb_all = jnp.cumsum(jax.nn.softmax(hg_lb_logits.astype(F32), axis=0), axis=0)
    hp = x_prompt.reshape(n_p, d)
    hs = x_sample.reshape(n_s, d)
    n_main = 7 * HG_WIDTH
    outs = [[] for _ in range(8)]
    for i in range(depth):
        final = i == depth - 1
        w_main = w_in[i, :, :n_main].astype(BF16)
        w_ff = jnp.pad(w_in[i, :, n_main:], ((0, 0), (0, LANES - FOX_HEADS))).astype(BF16)
        f_bias = jnp.pad(fox_f_bias[i], (0, LANES - FOX_HEADS)).reshape(1, LANES)
        g_mix_i = g_mix[i].reshape(1, d)
        lb_i = lb_all[i].reshape(1, HG_WIDTH)
        g_hg = g_hg_out[i].reshape(1, HG_WIDTH)
        g_fox = g_fox_out[i].reshape(1, FOX_WIDTH)
        w_out_i = w_out[i].astype(BF16)
        g_mlp_i = g_mlp[i].reshape(1, d)
        w_r_t = w_router[i].T
        b_r = b_router[i].reshape(N_EXPERTS, 1)

        q, f, v, gt, fq, fk, fv, kb, vb, lf, c = _inproj(hp, g_mix_i, w_main, w_ff, lb_i, f_bias, seq)
        oh, sp = _hgrn_prompt(q, f, v, gt, g_hg, bsz, seq)
        of = _fox_prompt(fq, kb, vb, c, c.T, g_fox, bsz, seq)
        h1_p, u_p, idx_p, gate_p = _outproj(oh, of, hp, w_out_i, g_mlp_i, w_r_t, b_r)
        outs[0].append(sp)
        outs[1].append(fk.reshape(bsz, seq, FOX_HEADS, FOX_DH))
        outs[2].append(fv.reshape(bsz, seq, FOX_HEADS, FOX_DH))
        outs[3].append(lf.reshape(bsz, seq, FOX_HEADS))

        q, f, v, gt, fq, fk, fv, kb, vb, lf, c = _inproj(hs, g_mix_i, w_main, w_ff, lb_i, f_bias, 1)
        oh, ss = _hgrn_step(q, f, v, gt, g_hg, state_hgrn[i])
        kv_t = lambda a: jnp.transpose(a[i], (0, 2, 3, 1)).reshape(n_phys, FOX_WIDTH, PAGE_SIZE)
        of = _fox_decode(fq, fk, fv, lf, g_fox, kv_t(cache_k), kv_t(cache_v),
                         jnp.transpose(cache_logf[i], (0, 2, 1)), page_table)
        h1_s, u_s, idx_s, gate_s = _outproj(oh, of, hs, w_out_i, g_mlp_i, w_r_t, b_r)
        outs[4].append(ss)
        outs[5].append(fk.reshape(dec_b, 1, FOX_HEADS, FOX_DH))
        outs[6].append(fv.reshape(dec_b, 1, FOX_HEADS, FOX_DH))
        outs[7].append(lf.reshape(dec_b, 1, FOX_HEADS))

        n_tile = -(-(n_p + n_s) // MOE_TILE) * MOE_TILE
        idx_all = jnp.concatenate(
            [idx_p, idx_s, jnp.full((TOP_K, n_tile - n_p - n_s), N_EXPERTS, jnp.int32)], axis=1)
        dest, blk_exp, n_used, cnt, pstart = _moe_plan(idx_all)
        xs = _moe_dispatch(u_p, u_s, dest, cnt, pstart)
        ys = _moe_experts(xs, blk_exp, n_used, w_up[i], b_up[i].reshape(N_EXPERTS, 1, -1),
                          w_down[i], b_down[i].reshape(N_EXPERTS, 1, -1))
        w_pg = w_ple_gate[i].astype(BF16)
        w_p = w_ple[i].astype(BF16)
        g_fin = g_final.reshape(1, d)
        hp = _combine(h1_p, p_prompt[i].reshape(n_p, -1), dest[:, :n_p], gate_p.T, ys, w_pg, w_p, g_fin, final)
        hs = _combine(h1_s, p_sample[i].reshape(n_s, -1), dest[:, n_p:n_p + n_s], gate_s.T, ys, w_pg, w_p, g_fin,
                      final)
    st = [jnp.stack(o) for o in outs]
    return (hp.reshape(bsz, seq, d), hs.reshape(dec_b, 1, d), *st)
```

```python
import functools

import jax
import jax.numpy as jnp
from jax import lax
from jax.experimental import pallas as pl
from jax.experimental.pallas import tpu as pltpu

F32 = jnp.float32
BF16 = jnp.bfloat16

HG_HEADS = 4
HG_DK = 128
HG_DV = 128
HG_WIDTH = HG_HEADS * HG_DK
FOX_HEADS = 8
FOX_DH = 64
FOX_WIDTH = FOX_HEADS * FOX_DH
N_EXPERTS = 32
TOP_K = 4
PAGE_SIZE = 128
NORM_EPS = 1e-6
SWIGLU_ALPHA = 1.702
SWIGLU_LIMIT = 7.0

LANES = 128
SUBLANES = 8
NEG_BIG = -1e30
VMEM_LIMIT = 56 * 1024 * 1024

HG_CHUNK = 16
HG_BLOCK = 128

def _params(*sem):
    return pltpu.CompilerParams(dimension_semantics=sem, vmem_limit_bytes=VMEM_LIMIT)


def _row_tile(m, pref):
    return pref if m % pref == 0 else m


def _split3(x):
    hi = x.astype(BF16)
    r1 = x - hi.astype(F32)
    mid = r1.astype(BF16)
    lo = (r1 - mid.astype(F32)).astype(BF16)
    return hi, mid, lo


def _dot01(mask_bf16, x):
    hi, mid, lo = _split3(x)
    d = lambda a: jnp.dot(mask_bf16, a, preferred_element_type=F32)
    return d(hi) + d(mid) + d(lo)


def _tri_lower(n, strict=False):
    r = lax.broadcasted_iota(jnp.int32, (n, n), 0)
    c = lax.broadcasted_iota(jnp.int32, (n, n), 1)
    m = (c < r) if strict else (c <= r)
    return jnp.where(m, 1.0, 0.0).astype(BF16)


def _sigmoid(x):
    return 1.0 / (1.0 + jnp.exp(-x))


def _silu(x):
    return x * _sigmoid(x)


def _log_sigmoid(x):
    return jnp.minimum(x, 0.0) - jnp.log1p(jnp.exp(-jnp.abs(x)))


def _inproj_kernel(x_ref, g_ref, w_ref, wf_ref, lb_ref, fb_ref,
                   q_ref, f_ref, i_ref, gt_ref, fq_ref, fk_ref, fv_ref, lf_ref, *rest, tiles_per_seq):
    x = x_ref[...]
    ms = jnp.mean(x * x, axis=-1, keepdims=True)
    u = (x * lax.rsqrt(ms + NORM_EPS) * g_ref[...]).astype(BF16)

    def proj(j):
        return jnp.dot(u, w_ref[:, j * HG_WIDTH:(j + 1) * HG_WIDTH], preferred_element_type=F32)

    q_ref[...] = _silu(proj(0))
    lb = lb_ref[...]
    f_ref[...] = lb + (1.0 - lb) * _sigmoid(proj(1))
    i_ref[...] = proj(2)
    gt_ref[...] = _silu(proj(3))
    fq_ref[...] = (proj(4) * (FOX_DH ** -0.5)).astype(BF16)
    fk = proj(5)
    fk_ref[...] = fk
    fv = proj(6)
    fv_ref[...] = fv
    ff = jnp.dot(u, wf_ref[...], preferred_element_type=F32) + fb_ref[...]
    lf = _log_sigmoid(ff)
    lf_ref[...] = lf[:, :FOX_HEADS]
    if tiles_per_seq:
        kb_ref, vb_ref, c_ref, carry_ref = rest
        kb_ref[...] = fk.astype(BF16)
        vb_ref[...] = fv.astype(BF16)
        tm = x.shape[0]

        @pl.when(pl.program_id(0) % tiles_per_seq == 0)
        def _():
            carry_ref[...] = jnp.zeros_like(carry_ref)

        c = _dot01(_tri_lower(tm), lf) + carry_ref[0:1, :]
        carry_ref[...] = jnp.broadcast_to(c[tm - 1:tm, :], carry_ref.shape)
        c_ref[...] = c[:, :FOX_HEADS]


def _inproj(x2d, g_mix, w_main, w_ff, lb, f_bias, seq_len):
    m, d = x2d.shape
    tm = _row_tile(m if seq_len == 1 else seq_len, 512)
    tiles_per_seq = 0 if seq_len == 1 else seq_len // tm
    row = lambda w: pl.BlockSpec((tm, w), lambda i: (i, 0))
    full = lambda a: pl.BlockSpec(a.shape, lambda i: (0,) * a.ndim)
    wide = jax.ShapeDtypeStruct((m, HG_WIDTH), F32)
    wide_bf = jax.ShapeDtypeStruct((m, HG_WIDTH), BF16)
    narrow = jax.ShapeDtypeStruct((m, FOX_HEADS), F32)
    out_specs = [row(HG_WIDTH)] * 7 + [row(FOX_HEADS)]
    out_shape = [wide, wide, wide, wide, wide_bf, wide, wide, narrow]
    scratch = []
    if tiles_per_seq:
        out_specs += [row(HG_WIDTH), row(HG_WIDTH), row(FOX_HEADS)]
        out_shape += [wide_bf, wide_bf, narrow]
        scratch = [pltpu.VMEM((SUBLANES, LANES), F32)]
    return pl.pallas_call(
        functools.partial(_inproj_kernel, tiles_per_seq=tiles_per_seq),
        grid=(m // tm,),
        in_specs=[row(d), full(g_mix), full(w_main), full(w_ff), full(lb), full(f_bias)],
        out_specs=out_specs,
        out_shape=out_shape,
        scratch_shapes=scratch,
        compiler_params=_params("arbitrary"),
        name="inproj",
    )(x2d, g_mix, w_main, w_ff, lb, f_bias)


def _hgrn_kernel(q_ref, f_ref, v_ref, gt_ref, gn_ref, o_ref, s_ref, s_scr, b_scr):
    tb = q_ref.shape[0]

    @pl.when(pl.program_id(1) == 0)
    def _():
        s_scr[...] = jnp.zeros_like(s_scr)

    tri = _tri_lower(HG_BLOCK)
    row_id = lax.broadcasted_iota(jnp.int32, (HG_CHUNK, 1), 0)
    ones_kk = jnp.ones((HG_DK, HG_DK), BF16)

    def block_body(blk, carry):
        r_blk = pl.multiple_of(blk * HG_BLOCK, HG_BLOCK)
        b_scr[...] = _dot01(tri, jnp.log(f_ref[pl.ds(r_blk, HG_BLOCK), :]))

        def chunk_body(ci, carry2):
            r_loc = pl.multiple_of(ci * HG_CHUNK, HG_CHUNK)
            r0 = r_blk + r_loc
            for h in range(HG_HEADS):
                cols = slice(h * HG_DK, (h + 1) * HG_DK)
                q = q_ref[pl.ds(r0, HG_CHUNK), cols]
                k = 1.0 - f_ref[pl.ds(r0, HG_CHUNK), cols]
                v = v_ref[pl.ds(r0, HG_CHUNK), cols]
                bb = b_scr[pl.ds(r_loc, HG_CHUNK), cols]
                b_prev = bb[0:1, :] - jnp.log(f_ref[pl.ds(r0, 1), cols])
                b = bb - b_prev
                st_old = s_scr[h]
                o = lax.dot_general((q * jnp.exp(b)).astype(BF16), st_old.astype(BF16), (((1,), (1,)), ((), ())),
                                    preferred_element_type=F32)
                terms, first_row = [], []
                for s in range(HG_CHUNK):
                    t0 = (s // SUBLANES) * SUBLANES
                    d = jnp.where(row_id[t0:] >= s, b[t0:] - b[s:s + 1, :], NEG_BIG)
                    terms.append(jnp.exp(d) * q[t0:] * k[s:s + 1, :])
                    first_row.append(t0)
                a_all = jnp.dot(jnp.concatenate(terms, axis=0).astype(BF16), ones_kk, preferred_element_type=F32)
                o_rows = [o[t:t + SUBLANES] for t in range(0, HG_CHUNK, SUBLANES)]
                at = 0
                for s in range(HG_CHUNK):
                    for g in range(first_row[s] // SUBLANES, HG_CHUNK // SUBLANES):
                        o_rows[g] = o_rows[g] + a_all[at:at + SUBLANES] * v[s:s + 1, :]
                        at += SUBLANES
                o = jnp.concatenate(o_rows, axis=0)
                b_last = b[HG_CHUNK - 1:HG_CHUNK, :]
                kd = (k * jnp.exp(b_last - b)).astype(BF16)
                upd_t = lax.dot_general(v.astype(BF16), kd, (((0,), (0,)), ((), ())),
                                        preferred_element_type=F32)
                s_scr[h] = st_old * jnp.exp(b_last) + upd_t
                ms = jnp.mean(o * o, axis=-1, keepdims=True)
                o = o * lax.rsqrt(ms + NORM_EPS) * gn_ref[:, cols] * gt_ref[pl.ds(r0, HG_CHUNK), cols]
                o_ref[pl.ds(r0, HG_CHUNK), cols] = o.astype(BF16)
            return carry2

        return lax.fori_loop(0, HG_BLOCK // HG_CHUNK, chunk_body, carry, unroll=2)

    lax.fori_loop(0, tb // HG_BLOCK, block_body, 0)

    @pl.when(pl.program_id(1) == pl.num_programs(1) - 1)
    def _():
        for h in range(HG_HEADS):
            s_ref[0, h] = jnp.transpose(s_scr[h])


def _hgrn_prompt(q, f, v, gt, g_norm, batch, seq):
    tb = _row_tile(seq, 512)
    assert tb % HG_BLOCK == 0
    nt = seq // tb
    row = pl.BlockSpec((tb, HG_WIDTH), lambda b, t: (b * nt + t, 0))
    return pl.pallas_call(
        _hgrn_kernel,
        grid=(batch, nt),
        in_specs=[row, row, row, row, pl.BlockSpec((1, HG_WIDTH), lambda b, t: (0, 0))],
        out_specs=[row, pl.BlockSpec((1, HG_HEADS, HG_DK, HG_DV), lambda b, t: (b, 0, 0, 0))],
        out_shape=[jax.ShapeDtypeStruct((batch * seq, HG_WIDTH), BF16),
                   jax.ShapeDtypeStruct((batch, HG_HEADS, HG_DK, HG_DV), F32)],
        scratch_shapes=[pltpu.VMEM((HG_HEADS, HG_DK, HG_DV), F32), pltpu.VMEM((HG_BLOCK, HG_WIDTH), F32)],
        compiler_params=_params("arbitrary", "arbitrary"),
        name="hgrn_prompt",
    )(q, f, v, gt, g_norm)


def _fox_prompt_kernel(q_ref, k_ref, v_ref, cq_ref, ck_ref, gn_ref, o_ref, m_scr, l_scr, acc_scr):
    pair = pl.program_id(1)
    qi = pl.program_id(2)
    ki = pl.program_id(3)
    tq = q_ref.shape[0]
    tk = k_ref.shape[0]

    @pl.when(ki == 0)
    def _():
        m_scr[...] = jnp.full_like(m_scr, NEG_BIG)
        l_scr[...] = jnp.zeros_like(l_scr)
        acc_scr[...] = jnp.zeros_like(acc_scr)

    @pl.when(ki <= qi)
    def _():
        lane = lax.broadcasted_iota(jnp.int32, (1, LANES), 1)
        first = lane < FOX_DH
        q = q_ref[...]
        k = k_ref[...]
        v = v_ref[...]
        q_pos = qi * tq + lax.broadcasted_iota(jnp.int32, (tq, tk), 0)
        k_pos = ki * tk + lax.broadcasted_iota(jnp.int32, (tq, tk), 1)
        causal = k_pos <= q_pos
        cq_all = cq_ref[...]
        ck_all = ck_ref[...]
        head_col = lax.broadcasted_iota(jnp.int32, (1, FOX_HEADS), 1)
        head_row = lax.broadcasted_iota(jnp.int32, (FOX_HEADS, 1), 0)
        pv = []
        for j in range(2):
            h = 2 * pair + j
            qh = jnp.where(first if j == 0 else ~first, q, jnp.zeros_like(q))
            s = lax.dot_general(qh, k, (((1,), (1,)), ((), ())), preferred_element_type=F32)
            cq = jnp.sum(jnp.where(head_col == h, cq_all, 0.0), axis=1, keepdims=True)
            ck = jnp.sum(jnp.where(head_row == h, ck_all, 0.0), axis=0, keepdims=True)
            s = jnp.where(causal, s + cq - ck, NEG_BIG)
            m_old = m_scr[j]
            m_new = jnp.maximum(m_old, jnp.max(s, axis=-1, keepdims=True))
            alpha = jnp.exp(m_old - m_new)
            p = jnp.exp(s - m_new)
            l_scr[j] = alpha * l_scr[j] + jnp.sum(p, axis=-1, keepdims=True)
            m_scr[j] = m_new
            pv.append((alpha, jnp.dot(p.astype(BF16), v, preferred_element_type=F32)))
        alpha = jnp.where(first, pv[0][0], pv[1][0])
        acc_scr[...] = alpha * acc_scr[...] + jnp.where(first, pv[0][1], pv[1][1])

    @pl.when(ki == qi)
    def _():
        lane = lax.broadcasted_iota(jnp.int32, (1, LANES), 1)
        first = lane < FOX_DH
        o = acc_scr[...] / jnp.where(first, l_scr[0], l_scr[1])
        sq = o * o
        ss0 = jnp.sum(jnp.where(first, sq, 0.0), axis=-1, keepdims=True)
        ss1 = jnp.sum(jnp.where(first, 0.0, sq), axis=-1, keepdims=True)
        ms = jnp.where(first, ss0, ss1) * (1.0 / FOX_DH)
        o_ref[...] = (o * lax.rsqrt(ms + NORM_EPS) * gn_ref[...]).astype(BF16)


def _fox_prompt(q_bf, k_bf, v_bf, c_col, c_row, g_norm, batch, seq):
    tq = _row_tile(seq, 512)
    nq = seq // tq
    pairs = FOX_HEADS // 2
    kv_map = lambda b, p, i, j: (b * nq + jnp.minimum(j, i), p)
    return pl.pallas_call(
        _fox_prompt_kernel,
        grid=(batch, pairs, nq, nq),
        in_specs=[pl.BlockSpec((tq, LANES), lambda b, p, i, j: (b * nq + i, p)),
                  pl.BlockSpec((tq, LANES), kv_map),
                  pl.BlockSpec((tq, LANES), kv_map),
                  pl.BlockSpec((tq, FOX_HEADS), lambda b, p, i, j: (b * nq + i, 0)),
                  pl.BlockSpec((FOX_HEADS, tq), lambda b, p, i, j: (0, b * nq + jnp.minimum(j, i))),
                  pl.BlockSpec((1, LANES), lambda b, p, i, j: (0, p))],
        out_specs=pl.BlockSpec((tq, LANES), lambda b, p, i, j: (b * nq + i, p)),
        out_shape=jax.ShapeDtypeStruct((batch * seq, FOX_WIDTH), BF16),
        scratch_shapes=[pltpu.VMEM((2, tq, 1), F32), pltpu.VMEM((2, tq, 1), F32),
                        pltpu.VMEM((tq, LANES), F32)],
        compiler_params=_params("arbitrary", "arbitrary", "arbitrary", "arbitrary"),
        name="fox_prompt",
    )(q_bf, k_bf, v_bf, c_col, c_row, g_norm)


def _hgrn_step_kernel(q_ref, f_ref, v_ref, gt_ref, gn_ref, s_ref, o_ref, so_ref):
    rows = q_ref.shape[0]

    def col(row):
        return jnp.transpose(jnp.broadcast_to(row, (HG_DK, HG_DK)))

    for r in range(rows):
        for h in range(HG_HEADS):
            cols = slice(h * HG_DK, (h + 1) * HG_DK)
            f_col = col(f_ref[r:r + 1, cols])
            q_col = col(q_ref[r:r + 1, cols])
            s_new = f_col * s_ref[r, h] + (1.0 - f_col) * v_ref[r:r + 1, cols]
            so_ref[r, h] = s_new
            o = jnp.sum(s_new * q_col, axis=0, keepdims=True)
            ms = jnp.mean(o * o, axis=-1, keepdims=True)
            o = o * lax.rsqrt(ms + NORM_EPS) * gn_ref[:, cols] * gt_ref[r:r + 1, cols]
            o_ref[r:r + 1, cols] = o.astype(BF16)


def _hgrn_step(q, f, v, gt, g_norm, state):
    n = q.shape[0]
    rows = SUBLANES
    assert n % rows == 0
    row = pl.BlockSpec((rows, HG_WIDTH), lambda i: (i, 0))
    st = pl.BlockSpec((rows, HG_HEADS, HG_DK, HG_DV), lambda i: (i, 0, 0, 0))
    return pl.pallas_call(
        _hgrn_step_kernel,
        grid=(n // rows,),
        in_specs=[row, row, row, row, pl.BlockSpec((1, HG_WIDTH), lambda i: (0, 0)), st],
        out_specs=[row, st],
        out_shape=[jax.ShapeDtypeStruct((n, HG_WIDTH), BF16), jax.ShapeDtypeStruct(state.shape, F32)],
        compiler_params=_params("arbitrary"),
        name="hgrn_step",
    )(q, f, v, gt, g_norm, state)


FOX_PAGES_PER_STEP = 16


def _dot01_rhs(x, mask_bf16):
    hi, mid, lo = _split3(x)
    d = lambda a: jnp.dot(a, mask_bf16, preferred_element_type=F32)
    return d(hi) + d(mid) + d(lo)


def _fox_decode_kernel(pt_ref, q_ref, kn_ref, vn_ref, lfn_ref, gn_ref, *refs):
    del pt_ref
    npg = FOX_PAGES_PER_STEP
    k_refs, v_refs, lf_refs = refs[:npg], refs[npg:2 * npg], refs[2 * npg:3 * npg]
    o_ref, m_scr, l_scr, r_scr, acc_scr = refs[3 * npg:]
    c = pl.program_id(1)
    head = lax.broadcasted_iota(jnp.int32, (FOX_HEADS, FOX_WIDTH), 0)
    lane = lax.broadcasted_iota(jnp.int32, (FOX_HEADS, FOX_WIDTH), 1)
    own = lane // FOX_DH == head
    q_rows = jnp.where(own, q_ref[0].astype(F32), 0.0)
    lf_new = lfn_ref[0]

    @pl.when(c == 0)
    def _():
        k_new = kn_ref[0].astype(BF16).astype(F32)
        m_scr[...] = jnp.sum(q_rows * k_new, axis=1, keepdims=True)
        l_scr[...] = jnp.ones_like(l_scr)
        r_scr[...] = jnp.zeros_like(r_scr)
        acc_scr[...] = jnp.broadcast_to(vn_ref[0], acc_scr.shape)

    q_bf = q_rows.astype(BF16)
    rr = lax.broadcasted_iota(jnp.int32, (PAGE_SIZE, PAGE_SIZE), 0)
    cc = lax.broadcasted_iota(jnp.int32, (PAGE_SIZE, PAGE_SIZE), 1)
    later = jnp.where(rr > cc, 1.0, 0.0).astype(BF16)

    run = r_scr[...]
    scores = []
    for g in range(npg):
        s = jnp.dot(q_bf, k_refs[g][0].astype(BF16), preferred_element_type=F32)
        lf = lf_refs[g][0]
        scores.append(s + _dot01_rhs(lf, later) + run + lf_new)
        run = run + jnp.sum(lf, axis=1, keepdims=True)
    r_scr[...] = run

    m_old = m_scr[...]
    m_new = m_old
    for s in scores:
        m_new = jnp.maximum(m_new, jnp.max(s, axis=1, keepdims=True))
    alpha = jnp.exp(m_old - m_new)
    l_new = alpha * l_scr[...]
    pv = jnp.zeros(acc_scr.shape, F32)
    for g in range(npg):
        p = jnp.exp(scores[g] - m_new)
        l_new = l_new + jnp.sum(p, axis=1, keepdims=True)
        pv = pv + lax.dot_general(p.astype(BF16), v_refs[g][0].astype(BF16), (((1,), (1,)), ((), ())),
                                  preferred_element_type=F32)
    m_scr[...] = m_new
    l_scr[...] = l_new
    acc_scr[...] = alpha * acc_scr[...] + pv

    @pl.when(c == pl.num_programs(1) - 1)
    def _():
        o = jnp.where(own, acc_scr[...] / l_scr[...], 0.0)
        ms = jnp.sum(o * o, axis=1, keepdims=True) * (1.0 / FOX_DH)
        o = o * lax.rsqrt(ms + NORM_EPS)
        o_ref[0] = (jnp.sum(o, axis=0, keepdims=True) * gn_ref[...]).astype(BF16)


def _fox_decode(q_bf, k_new, v_new, lf_new, g_norm, cache_kt, cache_vt, cache_lft, page_table):
    n, n_pages = page_table.shape
    npg = FOX_PAGES_PER_STEP
    assert n_pages % npg == 0
    n_chunks = n_pages // npg

    def page_map(g):
        return lambda b, c, pt: (pt[b, (n_chunks - 1 - c) * npg + (npg - 1 - g)], 0, 0)

    row3 = lambda h, w: pl.BlockSpec((1, h, w), lambda b, c, pt: (b, 0, 0))
    kv_specs = [pl.BlockSpec((1, FOX_WIDTH, PAGE_SIZE), page_map(g)) for g in range(npg)]
    lf_specs = [pl.BlockSpec((1, FOX_HEADS, PAGE_SIZE), page_map(g)) for g in range(npg)]
    stat = pltpu.VMEM((FOX_HEADS, 1), F32)
    grid_spec = pltpu.PrefetchScalarGridSpec(
        num_scalar_prefetch=1,
        grid=(n, n_chunks),
        in_specs=[row3(1, FOX_WIDTH), row3(1, FOX_WIDTH), row3(1, FOX_WIDTH), row3(FOX_HEADS, 1),
                  pl.BlockSpec((1, FOX_WIDTH), lambda b, c, pt: (0, 0))] + kv_specs + kv_specs + lf_specs,
        out_specs=row3(1, FOX_WIDTH),
        scratch_shapes=[stat, stat, stat, pltpu.VMEM((FOX_HEADS, FOX_WIDTH), F32)],
    )
    out = pl.pallas_call(
        _fox_decode_kernel,
        grid_spec=grid_spec,
        out_shape=jax.ShapeDtypeStruct((n, 1, FOX_WIDTH), BF16),
        compiler_params=_params("arbitrary", "arbitrary"),
        name="fox_decode",
    )(page_table, q_bf.reshape(n, 1, FOX_WIDTH), k_new.reshape(n, 1, FOX_WIDTH), v_new.reshape(n, 1, FOX_WIDTH),
      lf_new.reshape(n, FOX_HEADS, 1), g_norm, *([cache_kt] * npg), *([cache_vt] * npg), *([cache_lft] * npg))
    return out.reshape(n, FOX_WIDTH)


def _outproj_kernel(oh_ref, of_ref, x_ref, w_ref, g_ref, wr_ref, br_ref, h_ref, u_ref, idx_ref, gate_ref):
    mix_in = jnp.concatenate([oh_ref[...], of_ref[...]], axis=-1)
    h = x_ref[...] + jnp.dot(mix_in, w_ref[...], preferred_element_type=F32)
    h_ref[...] = h
    ms = jnp.mean(h * h, axis=-1, keepdims=True)
    u = h * lax.rsqrt(ms + NORM_EPS) * g_ref[...]
    _store_tile_rows(u_ref, u)
    vals = lax.dot_general(wr_ref[...], u, (((1,), (1,)), ((), ())), preferred_element_type=F32,
                           precision=lax.Precision.HIGHEST) + br_ref[...]
    expert = lax.broadcasted_iota(jnp.int32, vals.shape, 0)
    top_v, top_i = [], []
    for _ in range(TOP_K):
        m = jnp.max(vals, axis=0, keepdims=True)
        sel = jnp.min(jnp.where(vals == m, expert, N_EXPERTS), axis=0, keepdims=True)
        vals = jnp.where(expert == sel, -jnp.inf, vals)
        top_v.append(m)
        top_i.append(sel)
    ex = [jnp.exp(t - top_v[0]) for t in top_v]
    den = ex[0] + ex[1] + ex[2] + ex[3]
    idx_ref[...] = jnp.concatenate(top_i, axis=0)
    gate_ref[...] = jnp.concatenate([x / den for x in ex], axis=0)


def _outproj(oh, of, x, w_out, g_mlp, w_router_t, b_router):
    m, d = x.shape
    assert d == SUBLANES * LANES
    tm = _row_tile(m, 512)
    row = lambda w: pl.BlockSpec((tm, w), lambda i: (i, 0))
    full = lambda a: pl.BlockSpec(a.shape, lambda i: (0,) * a.ndim)
    colblk = pl.BlockSpec((TOP_K, tm), lambda i: (0, i))
    return pl.pallas_call(
        _outproj_kernel,
        grid=(m // tm,),
        in_specs=[row(HG_WIDTH), row(FOX_WIDTH), row(d), full(w_out), full(g_mlp), full(w_router_t),
                  full(b_router)],
        out_specs=[row(d), pl.BlockSpec((tm, SUBLANES, LANES), lambda i: (i, 0, 0)), colblk, colblk],
        out_shape=[jax.ShapeDtypeStruct((m, d), F32), jax.ShapeDtypeStruct((m, SUBLANES, LANES), F32),
                   jax.ShapeDtypeStruct((TOP_K, m), jnp.int32), jax.ShapeDtypeStruct((TOP_K, m), F32)],
        compiler_params=_params("arbitrary"),
        name="outproj",
    )(oh, of, x, w_out, g_mlp, w_router_t, b_router)


MOE_TILE = 512
MOE_BLK = 256


def _moe_plan_kernel(idx_ref, dest_ref, be_ref, nu_ref, cnt_ref, ps_ref, cnt_scr, run_scr, ps_scr):
    phase = pl.program_id(0)
    step = pl.program_id(1)
    tm = idx_ref.shape[1]
    expert = lax.broadcasted_iota(jnp.int32, (N_EXPERTS, tm), 0)
    idx = idx_ref[...]
    onehot = [expert == idx[j:j + 1, :] for j in range(TOP_K)]

    @pl.when((phase == 0) & (step == 0))
    def _():
        cnt_scr[...] = jnp.zeros_like(cnt_scr)

    @pl.when(phase == 0)
    def _():
        tot = jnp.zeros((N_EXPERTS, 1), F32)
        for oh in onehot:
            tot = tot + jnp.sum(jnp.where(oh, 1.0, 0.0), axis=1, keepdims=True)
        cnt_scr[...] = cnt_scr[...] + tot

    @pl.when((phase == 1) & (step == 0))
    def _():
        cnt = cnt_scr[...]
        padded = jnp.floor((cnt + (MOE_BLK - 1)) * (1.0 / MOE_BLK)) * MOE_BLK
        pstart = _dot01(_tri_lower(N_EXPERTS, strict=True), padded)
        pend = pstart + padded
        ps_scr[...] = pstart
        run_scr[...] = jnp.zeros_like(run_scr)
        cnt_ref[...] = cnt.astype(jnp.int32)
        ps_ref[...] = pstart.astype(jnp.int32)
        nb = be_ref.shape[1]
        row0 = lax.broadcasted_iota(jnp.int32, (1, nb), 1).astype(F32) * MOE_BLK
        be = jnp.sum(jnp.where(pend[:, 0:1] <= row0, 1.0, 0.0), axis=0, keepdims=True)
        be_ref[...] = jnp.minimum(be, N_EXPERTS - 1).astype(jnp.int32)
        total = jnp.max(pend[:, 0:1], axis=0, keepdims=True)
        nu_ref[...] = jnp.broadcast_to(total * (1.0 / MOE_BLK), nu_ref.shape).astype(jnp.int32)

    @pl.when(phase == 1)
    def _():
        r = lax.broadcasted_iota(jnp.int32, (tm, tm), 0)
        c = lax.broadcasted_iota(jnp.int32, (tm, tm), 1)
        before = jnp.where(r < c, 1.0, 0.0).astype(BF16)
        run = run_scr[:, 0:1]
        base = ps_scr[:, 0:1]
        rows = []
        for oh in onehot:
            ohf = jnp.where(oh, 1.0, 0.0)
            prefix = jnp.dot(ohf.astype(BF16), before, preferred_element_type=F32)
            rows.append(jnp.sum(ohf * (base + run + prefix), axis=0, keepdims=True))
            run = run + jnp.sum(ohf, axis=1, keepdims=True)
        run_scr[...] = jnp.broadcast_to(run, run_scr.shape)
        dest_ref[...] = jnp.concatenate(rows, axis=0).astype(jnp.int32)


def _moe_blocks(n_tok):
    return n_tok * TOP_K // MOE_BLK + N_EXPERTS


def _moe_plan(idx_all):
    n_tok = idx_all.shape[1]
    nb = -(-_moe_blocks(n_tok) // LANES) * LANES
    lane_i32 = jax.ShapeDtypeStruct((N_EXPERTS, LANES), jnp.int32)
    const = lambda shape: pl.BlockSpec(shape, lambda p, s: (0, 0))
    dest, be, nu, cnt, ps = pl.pallas_call(
        _moe_plan_kernel,
        grid=(2, n_tok // MOE_TILE),
        in_specs=[pl.BlockSpec((TOP_K, MOE_TILE), lambda p, s: (0, s))],
        out_specs=[pl.BlockSpec((TOP_K, MOE_TILE), lambda p, s: (0, s * p)),
                   const((1, nb)), const((1, LANES)), const((N_EXPERTS, LANES)), const((N_EXPERTS, LANES))],
        out_shape=[jax.ShapeDtypeStruct((TOP_K, n_tok), jnp.int32), jax.ShapeDtypeStruct((1, nb), jnp.int32),
                   jax.ShapeDtypeStruct((1, LANES), jnp.int32), lane_i32, lane_i32],
        scratch_shapes=[pltpu.VMEM((N_EXPERTS, LANES), F32)] * 3,
        compiler_params=_params("arbitrary", "arbitrary"),
        name="moe_plan",
    )(idx_all)
    return dest, be[0], nu[0, :1], cnt[:, 0], ps[:, 0]


def _row_copy(src, dst, sem):
    return pltpu.make_async_copy(src, dst, sem)


def _load_tile_rows(ref):
    return jnp.concatenate([ref[:, c, :] for c in range(SUBLANES)], axis=-1)


def _store_tile_rows(ref, val):
    for c in range(SUBLANES):
        ref[:, c, :] = val[:, c * LANES:(c + 1) * LANES]


def _moe_dispatch_kernel(cnt_ref, ps_ref, dest_ref, up_ref, us_ref, xs_ref, zero_scr, sem, *, n_prompt_tiles):
    i = pl.program_id(0)

    def scatter(src_ref):
        n = src_ref.shape[0]

        def start(t, carry):
            for j in range(TOP_K):
                _row_copy(src_ref.at[pl.ds(t, 1)], xs_ref.at[pl.ds(dest_ref[j, t], 1)], sem).start(priority=j % 2)
            return carry

        def wait(t, carry):
            for j in range(TOP_K):
                _row_copy(src_ref.at[pl.ds(0, 1)], xs_ref.at[pl.ds(0, 1)], sem).wait()
            return carry

        lax.fori_loop(0, n, start, 0, unroll=4)
        lax.fori_loop(0, n, wait, 0, unroll=4)

    @pl.when(i < n_prompt_tiles)
    def _():
        scatter(up_ref)

    @pl.when(i == n_prompt_tiles)
    def _():
        scatter(us_ref)

    @pl.when(i == n_prompt_tiles + 1)
    def _():
        zero_scr[...] = jnp.zeros_like(zero_scr)

        def fill(e, start_not_wait):
            cnt = cnt_ref[e]
            pad = (-cnt) & (MOE_BLK - 1)
            off = ps_ref[e] + cnt

            def one(r, carry):
                cp = _row_copy(zero_scr.at[pl.ds(0, 1)], xs_ref.at[pl.ds(off + r, 1)], sem)
                if start_not_wait:
                    cp.start()
                else:
                    cp.wait()
                return carry

            lax.fori_loop(0, pad, one, 0)

        lax.fori_loop(0, N_EXPERTS, lambda e, c: (fill(e, True), c)[1], 0)
        lax.fori_loop(0, N_EXPERTS, lambda e, c: (fill(e, False), c)[1], 0)

        last = N_EXPERTS - 1
        used = (ps_ref[last] + cnt_ref[last] + (MOE_BLK - 1)) // MOE_BLK
        n_blocks = xs_ref.shape[0] // MOE_BLK

        def tail(start_not_wait):
            def one(b, carry):
                row0 = pl.multiple_of(b * MOE_BLK, MOE_BLK)
                for r in range(0, MOE_BLK, SUBLANES):
                    cp = _row_copy(zero_scr, xs_ref.at[pl.ds(row0 + r, SUBLANES)], sem)
                    if start_not_wait:
                        cp.start()
                    else:
                        cp.wait()
                return carry

            lax.fori_loop(used, n_blocks, one, 0)

        tail(True)
        tail(False)


def _moe_dispatch(u_prompt, u_sample, dest, cnt, pstart):
    n_p = u_prompt.shape[0]
    tile_row = u_prompt.shape[1:]
    n_s = u_sample.shape[0]
    n_tok = dest.shape[1]
    assert n_p % MOE_TILE == 0 and n_s <= MOE_TILE and n_tok == n_p + MOE_TILE
    npt = n_p // MOE_TILE
    cap = _moe_blocks(n_tok) * MOE_BLK
    grid_spec = pltpu.PrefetchScalarGridSpec(
        num_scalar_prefetch=2,
        grid=(npt + 2,),
        in_specs=[pl.BlockSpec((TOP_K, MOE_TILE), lambda i, c, p: (0, jnp.minimum(i, npt)),
                               memory_space=pltpu.SMEM),
                  pl.BlockSpec((MOE_TILE,) + tile_row, lambda i, c, p: (jnp.minimum(i, npt - 1), 0, 0)),
                  pl.BlockSpec((n_s,) + tile_row, lambda i, c, p: (0, 0, 0))],
        out_specs=pl.BlockSpec(memory_space=pl.ANY),
        scratch_shapes=[pltpu.VMEM((SUBLANES,) + tile_row, F32), pltpu.SemaphoreType.DMA(())],
    )
    return pl.pallas_call(
        functools.partial(_moe_dispatch_kernel, n_prompt_tiles=npt),
        grid_spec=grid_spec,
        out_shape=jax.ShapeDtypeStruct((cap,) + tile_row, F32),
        compiler_params=_params("arbitrary"),
        name="moe_dispatch",
    )(cnt, pstart, dest, u_prompt, u_sample)


def _moe_experts_kernel(be_ref, nu_ref, x_ref, wu_ref, bu_ref, wd_ref, bd_ref, y_ref, wu_bf, wd_bf):
    d_ff = wd_ref.shape[1]
    r = pl.program_id(0)
    active = r < nu_ref[0]
    new_expert = (r == 0) | (be_ref[r] != be_ref[jnp.maximum(r - 1, 0)])

    @pl.when(active & new_expert)
    def _():
        wu_bf[...] = wu_ref[0].astype(BF16)
        wd_bf[...] = wd_ref[0].astype(BF16)

    @pl.when(active)
    def _():
        x = _load_tile_rows(x_ref).astype(BF16)
        hdn = jnp.dot(x, wu_bf[...], preferred_element_type=F32) + bu_ref[0]
        glu = jnp.minimum(hdn[:, :d_ff], SWIGLU_LIMIT)
        lin = jnp.clip(hdn[:, d_ff:], -SWIGLU_LIMIT, SWIGLU_LIMIT)
        act = glu * _sigmoid(SWIGLU_ALPHA * glu) * (lin + 1.0)
        _store_tile_rows(y_ref, jnp.dot(act.astype(BF16), wd_bf[...], preferred_element_type=F32) + bd_ref[0])

    @pl.when(pl.program_id(0) >= nu_ref[0])
    def _():
        y_ref[...] = jnp.zeros_like(y_ref)


def _moe_experts(xs, blk_exp, n_used, w_up, b_up, w_down, b_down):
    cap = xs.shape[0]
    tile_row = xs.shape[1:]
    nb = cap // MOE_BLK
    last = lambda r, nu: jnp.minimum(r, nu[0] - 1)
    w_map = lambda r, be, nu: (be[last(r, nu)], 0, 0)
    x_map = lambda r, be, nu: (last(r, nu), 0, 0)
    grid_spec = pltpu.PrefetchScalarGridSpec(
        num_scalar_prefetch=2,
        grid=(nb,),
        in_specs=[pl.BlockSpec((MOE_BLK,) + tile_row, x_map),
                  pl.BlockSpec((1,) + w_up.shape[1:], w_map), pl.BlockSpec((1,) + b_up.shape[1:], w_map),
                  pl.BlockSpec((1,) + w_down.shape[1:], w_map), pl.BlockSpec((1,) + b_down.shape[1:], w_map)],
        out_specs=pl.BlockSpec((MOE_BLK,) + tile_row, lambda r, be, nu: (r, 0, 0)),
        scratch_shapes=[pltpu.VMEM(w_up.shape[1:], BF16), pltpu.VMEM(w_down.shape[1:], BF16)],
    )
    return pl.pallas_call(
        _moe_experts_kernel,
        grid_spec=grid_spec,
        out_shape=jax.ShapeDtypeStruct(xs.shape, F32),
        compiler_params=_params("arbitrary"),
        name="moe_experts",
    )(blk_exp, n_used, xs, w_up, b_up, w_down, b_down)


def _combine_kernel(dest_ref, dest_next_ref, h_ref, p_ref, gate_ref, ys_ref, wg_ref, wp_ref, gf_ref, o_ref,
                    y_scr, sem, *, final):
    tm = h_ref.shape[0]
    i = pl.program_id(0)
    slot = i % 2

    def gather(idx_ref, s):
        def start(t, carry):
            for j in range(TOP_K):
                _row_copy(ys_ref.at[pl.ds(idx_ref[j, t], 1)], y_scr.at[s, j, pl.ds(t, 1)],
                          sem.at[s]).start(priority=j % 2)
            return carry

        lax.fori_loop(0, tm, start, 0, unroll=4)

    @pl.when(i == 0)
    def _():
        gather(dest_ref, slot)

    @pl.when(i + 1 < pl.num_programs(0))
    def _():
        gather(dest_next_ref, 1 - slot)

    def wait(t, carry):
        for j in range(TOP_K):
            _row_copy(ys_ref.at[pl.ds(0, 1)], y_scr.at[slot, 0, pl.ds(0, 1)], sem.at[slot]).wait()
        return carry

    lax.fori_loop(0, tm, wait, 0, unroll=4)
    gate = gate_ref[...]
    h = h_ref[...]
    cols = []
    for c in range(SUBLANES):
        hc = h[:, c * LANES:(c + 1) * LANES]
        for j in range(TOP_K):
            hc = hc + gate[:, j:j + 1] * y_scr[slot, j, :, c, :]
        cols.append(hc)
    h = jnp.concatenate(cols, axis=-1)
    emb_gate = _sigmoid(jnp.dot(h.astype(BF16), wg_ref[...], preferred_element_type=F32))
    emb = jnp.dot(p_ref[...].astype(BF16), wp_ref[...], preferred_element_type=F32)
    h = h + emb_gate * emb
    if final:
        ms = jnp.mean(h * h, axis=-1, keepdims=True)
        h = h * lax.rsqrt(ms + NORM_EPS) * gf_ref[...]
    o_ref[...] = h


def _combine(h1, p_emb, dest, gate_t, ys, w_gate, w_emb, g_final, final):
    m, d = h1.shape
    tm = _row_tile(m, 256)
    full = lambda a: pl.BlockSpec(a.shape, lambda i: (0,) * a.ndim)
    nt = m // tm
    return pl.pallas_call(
        functools.partial(_combine_kernel, final=final),
        grid=(nt,),
        in_specs=[pl.BlockSpec((TOP_K, tm), lambda i: (0, i), memory_space=pltpu.SMEM),
                  pl.BlockSpec((TOP_K, tm), lambda i: (0, jnp.minimum(i + 1, nt - 1)), memory_space=pltpu.SMEM),
                  pl.BlockSpec((tm, d), lambda i: (i, 0)),
                  pl.BlockSpec((tm, p_emb.shape[1]), lambda i: (i, 0)),
                  pl.BlockSpec((tm, TOP_K), lambda i: (i, 0)),
                  pl.BlockSpec(memory_space=pl.ANY),
                  full(w_gate), full(w_emb), full(g_final)],
        out_specs=pl.BlockSpec((tm, d), lambda i: (i, 0)),
        out_shape=jax.ShapeDtypeStruct((m, d), F32),
        scratch_shapes=[pltpu.VMEM((2, TOP_K, tm) + ys.shape[1:], F32), pltpu.SemaphoreType.DMA((2,))],
        compiler_params=_params("arbitrary"),
        name="combine",
    )(dest, dest, h1, p_emb, gate_t, ys, w_gate, w_emb, g_final)


def kernel(x_prompt, x_sample, state_hgrn, cache_k, cache_v, cache_logf, page_table, p_prompt, p_sample, g_mix, w_in, hg_lb_logits, g_hg_out, fox_f_bias, g_fox_out, w_out, g_mlp, w_router, b_router, w_up, b_up, w_down, b_down, w_ple, w_ple_gate, g_final):
    depth = w_in.shape[0]
    bsz, seq, d = x_prompt.shape
    dec_b = x_sample.shape[0]
    n_p, n_s = bsz * seq, dec_b
    n_phys = cache_k.shape[1]
    lb_all = jnp.cumsum(jax.nn.softmax(hg_lb_logits.astype(F32), axis=0), axis=0)
    hp = x_prompt.reshape(n_p, d)
    hs = x_sample.reshape(n_s, d)
    n_main = 7 * HG_WIDTH
    outs = [[] for _ in range(8)]
    for i in range(depth):
        final = i == depth - 1
        w_main = w_in[i, :, :n_main].astype(BF16)
        w_ff = jnp.pad(w_in[i, :, n_main:], ((0, 0), (0, LANES - FOX_HEADS))).astype(BF16)
        f_bias = jnp.pad(fox_f_bias[i], (0, LANES - FOX_HEADS)).reshape(1, LANES)
        g_mix_i = g_mix[i].reshape(1, d)
        lb_i = lb_all[i].reshape(1, HG_WIDTH)
        g_hg = g_hg_out[i].reshape(1, HG_WIDTH)
        g_fox = g_fox_out[i].reshape(1, FOX_WIDTH)
        w_out_i = w_out[i].astype(BF16)
        g_mlp_i = g_mlp[i].reshape(1, d)
        w_r_t = w_router[i].T
        b_r = b_router[i].reshape(N_EXPERTS, 1)

        q, f, v, gt, fq, fk, fv, lf, kb, vb, c = _inproj(hp, g_mix_i, w_main, w_ff, lb_i, f_bias, seq)
        oh, sp = _hgrn_prompt(q, f, v, gt, g_hg, bsz, seq)
        of = _fox_prompt(fq, kb, vb, c, c.T, g_fox, bsz, seq)
        h1_p, u_p, idx_p, gate_p = _outproj(oh, of, hp, w_out_i, g_mlp_i, w_r_t, b_r)
        outs[0].append(sp)
        outs[1].append(fk.reshape(bsz, seq, FOX_HEADS, FOX_DH))
        outs[2].append(fv.reshape(bsz, seq, FOX_HEADS, FOX_DH))
        outs[3].append(lf.reshape(bsz, seq, FOX_HEADS))

        q, f, v, gt, fq, fk, fv, lf = _inproj(hs, g_mix_i, w_main, w_ff, lb_i, f_bias, 1)
        oh, ss = _hgrn_step(q, f, v, gt, g_hg, state_hgrn[i])
        kv_t = lambda a: jnp.transpose(a[i], (0, 2, 3, 1)).reshape(n_phys, FOX_WIDTH, PAGE_SIZE)
        of = _fox_decode(fq, fk, fv, lf, g_fox, kv_t(cache_k), kv_t(cache_v),
                         jnp.transpose(cache_logf[i], (0, 2, 1)), page_table)
        h1_s, u_s, idx_s, gate_s = _outproj(oh, of, hs, w_out_i, g_mlp_i, w_r_t, b_r)
        outs[4].append(ss)
        outs[5].append(fk.reshape(dec_b, 1, FOX_HEADS, FOX_DH))
        outs[6].append(fv.reshape(dec_b, 1, FOX_HEADS, FOX_DH))
        outs[7].append(lf.reshape(dec_b, 1, FOX_HEADS))

        n_tile = -(-(n_p + n_s) // MOE_TILE) * MOE_TILE
        idx_all = jnp.concatenate(
            [idx_p, idx_s, jnp.full((TOP_K, n_tile - n_p - n_s), N_EXPERTS, jnp.int32)], axis=1)
        dest, blk_exp, n_used, cnt, pstart = _moe_plan(idx_all)
        xs = _moe_dispatch(u_p, u_s, dest, cnt, pstart)
        ys = _moe_experts(xs, blk_exp, n_used, w_up[i], b_up[i].reshape(N_EXPERTS, 1, -1),
                          w_down[i], b_down[i].reshape(N_EXPERTS, 1, -1))
        w_pg = w_ple_gate[i].astype(BF16)
        w_p = w_ple[i].astype(BF16)
        g_fin = g_final.reshape(1, d)
        hp = _combine(h1_p, p_prompt[i].reshape(n_p, -1), dest[:, :n_p], gate_p.T, ys, w_pg, w_p, g_fin, final)
        hs = _combine(h1_s, p_sample[i].reshape(n_s, -1), dest[:, n_p:n_p + n_s], gate_s.T, ys, w_pg, w_p, g_fin,
                      final)
    st = [jnp.stack(o) for o in outs]
    return (hp.reshape(bsz, seq, d), hs.reshape(dec_b, 1, d), *st)
```

```python
import functools

import jax
import jax.numpy as jnp
from jax import lax
from jax.experimental import pallas as pl
from jax.experimental.pallas import tpu as pltpu

F32 = jnp.float32
BF16 = jnp.bfloat16

HG_HEADS = 4
HG_DK = 128
HG_DV = 128
HG_WIDTH = HG_HEADS * HG_DK
FOX_HEADS = 8
FOX_DH = 64
FOX_WIDTH = FOX_HEADS * FOX_DH
N_EXPERTS = 32
TOP_K = 4
PAGE_SIZE = 128
NORM_EPS = 1e-6
SWIGLU_ALPHA = 1.702
SWIGLU_LIMIT = 7.0

LANES = 128
SUBLANES = 8
NEG_BIG = -1e30
VMEM_LIMIT = 56 * 1024 * 1024

HG_CHUNK = 16
HG_BLOCK = 128

def _params(*sem):
    return pltpu.CompilerParams(dimension_semantics=sem, vmem_limit_bytes=VMEM_LIMIT)


def _row_tile(m, pref):
    return pref if m % pref == 0 else m


def _split3(x):
    hi = x.astype(BF16)
    r1 = x - hi.astype(F32)
    mid = r1.astype(BF16)
    lo = (r1 - mid.astype(F32)).astype(BF16)
    return hi, mid, lo


def _dot01(mask_bf16, x):
    hi, mid, lo = _split3(x)
    d = lambda a: jnp.dot(mask_bf16, a, preferred_element_type=F32)
    return d(hi) + d(mid) + d(lo)


def _tri_lower(n, strict=False):
    r = lax.broadcasted_iota(jnp.int32, (n, n), 0)
    c = lax.broadcasted_iota(jnp.int32, (n, n), 1)
    m = (c < r) if strict else (c <= r)
    return jnp.where(m, 1.0, 0.0).astype(BF16)


def _sigmoid(x):
    return 1.0 / (1.0 + jnp.exp(-x))


def _silu(x):
    return x * _sigmoid(x)


def _log_sigmoid(x):
    return jnp.minimum(x, 0.0) - jnp.log1p(jnp.exp(-jnp.abs(x)))


def _inproj_kernel(x_ref, g_ref, w_ref, wf_ref, lb_ref, fb_ref,
                   q_ref, f_ref, i_ref, gt_ref, fq_ref, fk_ref, fv_ref, lf_ref, *rest, tiles_per_seq):
    x = x_ref[...]
    ms = jnp.mean(x * x, axis=-1, keepdims=True)
    u = (x * lax.rsqrt(ms + NORM_EPS) * g_ref[...]).astype(BF16)

    def proj(j):
        return jnp.dot(u, w_ref[:, j * HG_WIDTH:(j + 1) * HG_WIDTH], preferred_element_type=F32)

    q_ref[...] = _silu(proj(0))
    lb = lb_ref[...]
    f_ref[...] = lb + (1.0 - lb) * _sigmoid(proj(1))
    i_ref[...] = proj(2)
    gt_ref[...] = _silu(proj(3))
    fq_ref[...] = (proj(4) * (FOX_DH ** -0.5)).astype(BF16)
    fk = proj(5)
    fk_ref[...] = fk
    fv = proj(6)
    fv_ref[...] = fv
    ff = jnp.dot(u, wf_ref[...], preferred_element_type=F32) + fb_ref[...]
    lf = _log_sigmoid(ff)
    lf_ref[...] = lf[:, :FOX_HEADS]
    if tiles_per_seq:
        kb_ref, vb_ref, c_ref, carry_ref = rest
        kb_ref[...] = fk.astype(BF16)
        vb_ref[...] = fv.astype(BF16)
        tm = x.shape[0]

        @pl.when(pl.program_id(0) % tiles_per_seq == 0)
        def _():
            carry_ref[...] = jnp.zeros_like(carry_ref)

        c = _dot01(_tri_lower(tm), lf) + carry_ref[0:1, :]
        carry_ref[...] = jnp.broadcast_to(c[tm - 1:tm, :], carry_ref.shape)
        c_ref[...] = c[:, :FOX_HEADS]


def _inproj(x2d, g_mix, w_main, w_ff, lb, f_bias, seq_len):
    m, d = x2d.shape
    tm = _row_tile(m if seq_len == 1 else seq_len, 512)
    tiles_per_seq = 0 if seq_len == 1 else seq_len // tm
    row = lambda w: pl.BlockSpec((tm, w), lambda i: (i, 0))
    full = lambda a: pl.BlockSpec(a.shape, lambda i: (0,) * a.ndim)
    wide = jax.ShapeDtypeStruct((m, HG_WIDTH), F32)
    wide_bf = jax.ShapeDtypeStruct((m, HG_WIDTH), BF16)
    narrow = jax.ShapeDtypeStruct((m, FOX_HEADS), F32)
    out_specs = [row(HG_WIDTH)] * 7 + [row(FOX_HEADS)]
    out_shape = [wide, wide, wide, wide, wide_bf, wide, wide, narrow]
    scratch = []
    if tiles_per_seq:
        out_specs += [row(HG_WIDTH), row(HG_WIDTH), row(FOX_HEADS)]
        out_shape += [wide_bf, wide_bf, narrow]
        scratch = [pltpu.VMEM((SUBLANES, LANES), F32)]
    return pl.pallas_call(
        functools.partial(_inproj_kernel, tiles_per_seq=tiles_per_seq),
        grid=(m // tm,),
        in_specs=[row(d), full(g_mix), full(w_main), full(w_ff), full(lb), full(f_bias)],
        out_specs=out_specs,
        out_shape=out_shape,
        scratch_shapes=scratch,
        compiler_params=_params("arbitrary"),
        name="inproj",
    )(x2d, g_mix, w_main, w_ff, lb, f_bias)


def _hgrn_kernel(q_ref, f_ref, v_ref, gt_ref, gn_ref, o_ref, s_ref, s_scr, b_scr):
    tb = q_ref.shape[0]

    @pl.when(pl.program_id(1) == 0)
    def _():
        s_scr[...] = jnp.zeros_like(s_scr)

    tri = _tri_lower(HG_BLOCK)
    row_id = lax.broadcasted_iota(jnp.int32, (HG_CHUNK, 1), 0)
    ones_kk = jnp.ones((HG_DK, HG_DK), BF16)

    def block_body(blk, carry):
        r_blk = pl.multiple_of(blk * HG_BLOCK, HG_BLOCK)
        b_scr[...] = _dot01(tri, jnp.log(f_ref[pl.ds(r_blk, HG_BLOCK), :]))

        def chunk_body(ci, carry2):
            r_loc = pl.multiple_of(ci * HG_CHUNK, HG_CHUNK)
            r0 = r_blk + r_loc
            for h in range(HG_HEADS):
                cols = slice(h * HG_DK, (h + 1) * HG_DK)
                q = q_ref[pl.ds(r0, HG_CHUNK), cols]
                k = 1.0 - f_ref[pl.ds(r0, HG_CHUNK), cols]
                v = v_ref[pl.ds(r0, HG_CHUNK), cols]
                bb = b_scr[pl.ds(r_loc, HG_CHUNK), cols]
                b_prev = bb[0:1, :] - jnp.log(f_ref[pl.ds(r0, 1), cols])
                b = bb - b_prev
                st_old = s_scr[h]
                o = lax.dot_general((q * jnp.exp(b)).astype(BF16), st_old.astype(BF16), (((1,), (1,)), ((), ())),
                                    preferred_element_type=F32)
                terms, first_row = [], []
                for s in range(HG_CHUNK):
                    t0 = (s // SUBLANES) * SUBLANES
                    d = jnp.where(row_id[t0:] >= s, b[t0:] - b[s:s + 1, :], NEG_BIG)
                    terms.append(jnp.exp(d) * q[t0:] * k[s:s + 1, :])
                    first_row.append(t0)
                a_all = jnp.dot(jnp.concatenate(terms, axis=0).astype(BF16), ones_kk, preferred_element_type=F32)
                o_rows = [o[t:t + SUBLANES] for t in range(0, HG_CHUNK, SUBLANES)]
                at = 0
                for s in range(HG_CHUNK):
                    for g in range(first_row[s] // SUBLANES, HG_CHUNK // SUBLANES):
                        o_rows[g] = o_rows[g] + a_all[at:at + SUBLANES] * v[s:s + 1, :]
                        at += SUBLANES
                o = jnp.concatenate(o_rows, axis=0)
                b_last = b[HG_CHUNK - 1:HG_CHUNK, :]
                kd = (k * jnp.exp(b_last - b)).astype(BF16)
                upd_t = lax.dot_general(v.astype(BF16), kd, (((0,), (0,)), ((), ())),
                                        preferred_element_type=F32)
                s_scr[h] = st_old * jnp.exp(b_last) + upd_t
                ms = jnp.mean(o * o, axis=-1, keepdims=True)
                o = o * lax.rsqrt(ms + NORM_EPS) * gn_ref[:, cols] * gt_ref[pl.ds(r0, HG_CHUNK), cols]
                o_ref[pl.ds(r0, HG_CHUNK), cols] = o.astype(BF16)
            return carry2

        return lax.fori_loop(0, HG_BLOCK // HG_CHUNK, chunk_body, carry, unroll=2)

    lax.fori_loop(0, tb // HG_BLOCK, block_body, 0)

    @pl.when(pl.program_id(1) == pl.num_programs(1) - 1)
    def _():
        for h in range(HG_HEADS):
            s_ref[0, h] = jnp.transpose(s_scr[h])


def _hgrn_prompt(q, f, v, gt, g_norm, batch, seq):
    tb = _row_tile(seq, 512)
    assert tb % HG_BLOCK == 0
    nt = seq // tb
    row = pl.BlockSpec((tb, HG_WIDTH), lambda b, t: (b * nt + t, 0))
    return pl.pallas_call(
        _hgrn_kernel,
        grid=(batch, nt),
        in_specs=[row, row, row, row, pl.BlockSpec((1, HG_WIDTH), lambda b, t: (0, 0))],
        out_specs=[row, pl.BlockSpec((1, HG_HEADS, HG_DK, HG_DV), lambda b, t: (b, 0, 0, 0))],
        out_shape=[jax.ShapeDtypeStruct((batch * seq, HG_WIDTH), BF16),
                   jax.ShapeDtypeStruct((batch, HG_HEADS, HG_DK, HG_DV), F32)],
        scratch_shapes=[pltpu.VMEM((HG_HEADS, HG_DK, HG_DV), F32), pltpu.VMEM((HG_BLOCK, HG_WIDTH), F32)],
        compiler_params=_params("arbitrary", "arbitrary"),
        name="hgrn_prompt",
    )(q, f, v, gt, g_norm)


def _fox_prompt_kernel(q_ref, k_ref, v_ref, cq_ref, ck_ref, gn_ref, o_ref, m_scr, l_scr, acc_scr):
    pair = pl.program_id(1)
    qi = pl.program_id(2)
    ki = pl.program_id(3)
    tq = q_ref.shape[0]
    tk = k_ref.shape[0]

    @pl.when(ki == 0)
    def _():
        m_scr[...] = jnp.full_like(m_scr, NEG_BIG)
        l_scr[...] = jnp.zeros_like(l_scr)
        acc_scr[...] = jnp.zeros_like(acc_scr)

    @pl.when(ki <= qi)
    def _():
        lane = lax.broadcasted_iota(jnp.int32, (1, LANES), 1)
        first = lane < FOX_DH
        q = q_ref[...]
        k = k_ref[...]
        v = v_ref[...]
        q_pos = qi * tq + lax.broadcasted_iota(jnp.int32, (tq, tk), 0)
        k_pos = ki * tk + lax.broadcasted_iota(jnp.int32, (tq, tk), 1)
        causal = k_pos <= q_pos
        cq_all = cq_ref[...]
        ck_all = ck_ref[...]
        head_col = lax.broadcasted_iota(jnp.int32, (1, FOX_HEADS), 1)
        head_row = lax.broadcasted_iota(jnp.int32, (FOX_HEADS, 1), 0)
        pv = []
        for j in range(2):
            h = 2 * pair + j
            qh = jnp.where(first if j == 0 else ~first, q, jnp.zeros_like(q))
            s = lax.dot_general(qh, k, (((1,), (1,)), ((), ())), preferred_element_type=F32)
            cq = jnp.sum(jnp.where(head_col == h, cq_all, 0.0), axis=1, keepdims=True)
            ck = jnp.sum(jnp.where(head_row == h, ck_all, 0.0), axis=0, keepdims=True)
            s = jnp.where(causal, s + cq - ck, NEG_BIG)
            m_old = m_scr[j]
            m_new = jnp.maximum(m_old, jnp.max(s, axis=-1, keepdims=True))
            alpha = jnp.exp(m_old - m_new)
            p = jnp.exp(s - m_new)
            l_scr[j] = alpha * l_scr[j] + jnp.sum(p, axis=-1, keepdims=True)
            m_scr[j] = m_new
            pv.append((alpha, jnp.dot(p.astype(BF16), v, preferred_element_type=F32)))
        alpha = jnp.where(first, pv[0][0], pv[1][0])
        acc_scr[...] = alpha * acc_scr[...] + jnp.where(first, pv[0][1], pv[1][1])

    @pl.when(ki == qi)
    def _():
        lane = lax.broadcasted_iota(jnp.int32, (1, LANES), 1)
        first = lane < FOX_DH
        o = acc_scr[...] / jnp.where(first, l_scr[0], l_scr[1])
        sq = o * o
        ss0 = jnp.sum(jnp.where(first, sq, 0.0), axis=-1, keepdims=True)
        ss1 = jnp.sum(jnp.where(first, 0.0, sq), axis=-1, keepdims=True)
        ms = jnp.where(first, ss0, ss1) * (1.0 / FOX_DH)
        o_ref[...] = (o * lax.rsqrt(ms + NORM_EPS) * gn_ref[...]).astype(BF16)


def _fox_prompt(q_bf, k_bf, v_bf, c_col, c_row, g_norm, batch, seq):
    tq = _row_tile(seq, 512)
    nq = seq // tq
    pairs = FOX_HEADS // 2
    kv_map = lambda b, p, i, j: (b * nq + jnp.minimum(j, i), p)
    return pl.pallas_call(
        _fox_prompt_kernel,
        grid=(batch, pairs, nq, nq),
        in_specs=[pl.BlockSpec((tq, LANES), lambda b, p, i, j: (b * nq + i, p)),
                  pl.BlockSpec((tq, LANES), kv_map),
                  pl.BlockSpec((tq, LANES), kv_map),
                  pl.BlockSpec((tq, FOX_HEADS), lambda b, p, i, j: (b * nq + i, 0)),
                  pl.BlockSpec((FOX_HEADS, tq), lambda b, p, i, j: (0, b * nq + jnp.minimum(j, i))),
                  pl.BlockSpec((1, LANES), lambda b, p, i, j: (0, p))],
        out_specs=pl.BlockSpec((tq, LANES), lambda b, p, i, j: (b * nq + i, p)),
        out_shape=jax.ShapeDtypeStruct((batch * seq, FOX_WIDTH), BF16),
        scratch_shapes=[pltpu.VMEM((2, tq, 1), F32), pltpu.VMEM((2, tq, 1), F32),
                        pltpu.VMEM((tq, LANES), F32)],
        compiler_params=_params("arbitrary", "arbitrary", "arbitrary", "arbitrary"),
        name="fox_prompt",
    )(q_bf, k_bf, v_bf, c_col, c_row, g_norm)


def _hgrn_step_kernel(q_ref, f_ref, v_ref, gt_ref, gn_ref, s_ref, o_ref, so_ref):
    rows = q_ref.shape[0]

    def col(row):
        return jnp.transpose(jnp.broadcast_to(row, (HG_DK, HG_DK)))

    for r in range(rows):
        for h in range(HG_HEADS):
            cols = slice(h * HG_DK, (h + 1) * HG_DK)
            f_col = col(f_ref[r:r + 1, cols])
            q_col = col(q_ref[r:r + 1, cols])
            s_new = f_col * s_ref[r, h] + (1.0 - f_col) * v_ref[r:r + 1, cols]
            so_ref[r, h] = s_new
            o = jnp.sum(s_new * q_col, axis=0, keepdims=True)
            ms = jnp.mean(o * o, axis=-1, keepdims=True)
            o = o * lax.rsqrt(ms + NORM_EPS) * gn_ref[:, cols] * gt_ref[r:r + 1, cols]
            o_ref[r:r + 1, cols] = o.astype(BF16)


def _hgrn_step(q, f, v, gt, g_norm, state):
    n = q.shape[0]
    rows = SUBLANES
    assert n % rows == 0
    row = pl.BlockSpec((rows, HG_WIDTH), lambda i: (i, 0))
    st = pl.BlockSpec((rows, HG_HEADS, HG_DK, HG_DV), lambda i: (i, 0, 0, 0))
    return pl.pallas_call(
        _hgrn_step_kernel,
        grid=(n // rows,),
        in_specs=[row, row, row, row, pl.BlockSpec((1, HG_WIDTH), lambda i: (0, 0)), st],
        out_specs=[row, st],
        out_shape=[jax.ShapeDtypeStruct((n, HG_WIDTH), BF16), jax.ShapeDtypeStruct(state.shape, F32)],
        compiler_params=_params("arbitrary"),
        name="hgrn_step",
    )(q, f, v, gt, g_norm, state)


FOX_PAGES_PER_STEP = 16


def _dot01_rhs(x, mask_bf16):
    hi, mid, lo = _split3(x)
    d = lambda a: jnp.dot(a, mask_bf16, preferred_element_type=F32)
    return d(hi) + d(mid) + d(lo)


def _fox_decode_kernel(pt_ref, q_ref, kn_ref, vn_ref, lfn_ref, gn_ref, *refs):
    del pt_ref
    npg = FOX_PAGES_PER_STEP
    k_refs, v_refs, lf_refs = refs[:npg], refs[npg:2 * npg], refs[2 * npg:3 * npg]
    o_ref, m_scr, l_scr, r_scr, acc_scr = refs[3 * npg:]
    c = pl.program_id(1)
    head = lax.broadcasted_iota(jnp.int32, (FOX_HEADS, FOX_WIDTH), 0)
    lane = lax.broadcasted_iota(jnp.int32, (FOX_HEADS, FOX_WIDTH), 1)
    own = lane // FOX_DH == head
    q_rows = jnp.where(own, q_ref[0].astype(F32), 0.0)
    lf_new = lfn_ref[0]

    @pl.when(c == 0)
    def _():
        k_new = kn_ref[0].astype(BF16).astype(F32)
        m_scr[...] = jnp.sum(q_rows * k_new, axis=1, keepdims=True)
        l_scr[...] = jnp.ones_like(l_scr)
        r_scr[...] = jnp.zeros_like(r_scr)
        acc_scr[...] = jnp.broadcast_to(vn_ref[0], acc_scr.shape)

    q_bf = q_rows.astype(BF16)
    rr = lax.broadcasted_iota(jnp.int32, (PAGE_SIZE, PAGE_SIZE), 0)
    cc = lax.broadcasted_iota(jnp.int32, (PAGE_SIZE, PAGE_SIZE), 1)
    later = jnp.where(rr > cc, 1.0, 0.0).astype(BF16)

    run = r_scr[...]
    scores = []
    for g in range(npg):
        s = jnp.dot(q_bf, k_refs[g][0].astype(BF16), preferred_element_type=F32)
        lf = lf_refs[g][0]
        scores.append(s + _dot01_rhs(lf, later) + run + lf_new)
        run = run + jnp.sum(lf, axis=1, keepdims=True)
    r_scr[...] = run

    m_old = m_scr[...]
    m_new = m_old
    for s in scores:
        m_new = jnp.maximum(m_new, jnp.max(s, axis=1, keepdims=True))
    alpha = jnp.exp(m_old - m_new)
    l_new = alpha * l_scr[...]
    pv = jnp.zeros(acc_scr.shape, F32)
    for g in range(npg):
        p = jnp.exp(scores[g] - m_new)
        l_new = l_new + jnp.sum(p, axis=1, keepdims=True)
        pv = pv + lax.dot_general(p.astype(BF16), v_refs[g][0].astype(BF16), (((1,), (1,)), ((), ())),
                                  preferred_element_type=F32)
    m_scr[...] = m_new
    l_scr[...] = l_new
    acc_scr[...] = alpha * acc_scr[...] + pv

    @pl.when(c == pl.num_programs(1) - 1)
    def _():
        o = jnp.where(own, acc_scr[...] / l_scr[...], 0.0)
        ms = jnp.sum(o * o, axis=1, keepdims=True) * (1.0 / FOX_DH)
        o = o * lax.rsqrt(ms + NORM_EPS)
        o_ref[0] = (jnp.sum(o, axis=0, keepdims=True) * gn_ref[...]).astype(BF16)


def _fox_decode(q_bf, k_new, v_new, lf_new, g_norm, cache_kt, cache_vt, cache_lft, page_table):
    n, n_pages = page_table.shape
    npg = FOX_PAGES_PER_STEP
    assert n_pages % npg == 0
    n_chunks = n_pages // npg

    def page_map(g):
        return lambda b, c, pt: (pt[b, (n_chunks - 1 - c) * npg + (npg - 1 - g)], 0, 0)

    row3 = lambda h, w: pl.BlockSpec((1, h, w), lambda b, c, pt: (b, 0, 0))
    kv_specs = [pl.BlockSpec((1, FOX_WIDTH, PAGE_SIZE), page_map(g)) for g in range(npg)]
    lf_specs = [pl.BlockSpec((1, FOX_HEADS, PAGE_SIZE), page_map(g)) for g in range(npg)]
    stat = pltpu.VMEM((FOX_HEADS, 1), F32)
    grid_spec = pltpu.PrefetchScalarGridSpec(
        num_scalar_prefetch=1,
        grid=(n, n_chunks),
        in_specs=[row3(1, FOX_WIDTH), row3(1, FOX_WIDTH), row3(1, FOX_WIDTH), row3(FOX_HEADS, 1),
                  pl.BlockSpec((1, FOX_WIDTH), lambda b, c, pt: (0, 0))] + kv_specs + kv_specs + lf_specs,
        out_specs=row3(1, FOX_WIDTH),
        scratch_shapes=[stat, stat, stat, pltpu.VMEM((FOX_HEADS, FOX_WIDTH), F32)],
    )
    out = pl.pallas_call(
        _fox_decode_kernel,
        grid_spec=grid_spec,
        out_shape=jax.ShapeDtypeStruct((n, 1, FOX_WIDTH), BF16),
        compiler_params=_params("arbitrary", "arbitrary"),
        name="fox_decode",
    )(page_table, q_bf.reshape(n, 1, FOX_WIDTH), k_new.reshape(n, 1, FOX_WIDTH), v_new.reshape(n, 1, FOX_WIDTH),
      lf_new.reshape(n, FOX_HEADS, 1), g_norm, *([cache_kt] * npg), *([cache_vt] * npg), *([cache_lft] * npg))
    return out.reshape(n, FOX_WIDTH)


def _outproj_kernel(oh_ref, of_ref, x_ref, w_ref, g_ref, wr_ref, br_ref, h_ref, u_ref, idx_ref, gate_ref):
    mix_in = jnp.concatenate([oh_ref[...], of_ref[...]], axis=-1)
    h = x_ref[...] + jnp.dot(mix_in, w_ref[...], preferred_element_type=F32)
    h_ref[...] = h
    ms = jnp.mean(h * h, axis=-1, keepdims=True)
    u = h * lax.rsqrt(ms + NORM_EPS) * g_ref[...]
    _store_tile_rows(u_ref, u)
    vals = lax.dot_general(wr_ref[...], u, (((1,), (1,)), ((), ())), preferred_element_type=F32,
                           precision=lax.Precision.HIGHEST) + br_ref[...]
    expert = lax.broadcasted_iota(jnp.int32, vals.shape, 0)
    top_v, top_i = [], []
    for _ in range(TOP_K):
        m = jnp.max(vals, axis=0, keepdims=True)
        sel = jnp.min(jnp.where(vals == m, expert, N_EXPERTS), axis=0, keepdims=True)
        vals = jnp.where(expert == sel, -jnp.inf, vals)
        top_v.append(m)
        top_i.append(sel)
    ex = [jnp.exp(t - top_v[0]) for t in top_v]
    den = ex[0] + ex[1] + ex[2] + ex[3]
    idx_ref[...] = jnp.concatenate(top_i, axis=0)
    gate_ref[...] = jnp.concatenate([x / den for x in ex], axis=0)


def _outproj(oh, of, x, w_out, g_mlp, w_router_t, b_router):
    m, d = x.shape
    assert d == SUBLANES * LANES
    tm = _row_tile(m, 512)
    row = lambda w: pl.BlockSpec((tm, w), lambda i: (i, 0))
    full = lambda a: pl.BlockSpec(a.shape, lambda i: (0,) * a.ndim)
    colblk = pl.BlockSpec((TOP_K, tm), lambda i: (0, i))
    return pl.pallas_call(
        _outproj_kernel,
        grid=(m // tm,),
        in_specs=[row(HG_WIDTH), row(FOX_WIDTH), row(d), full(w_out), full(g_mlp), full(w_router_t),
                  full(b_router)],
        out_specs=[row(d), pl.BlockSpec((tm * SUBLANES, LANES), lambda i: (i, 0)), colblk, colblk],
        out_shape=[jax.ShapeDtypeStruct((m, d), F32), jax.ShapeDtypeStruct((m * SUBLANES, LANES), F32),
                   jax.ShapeDtypeStruct((TOP_K, m), jnp.int32), jax.ShapeDtypeStruct((TOP_K, m), F32)],
        compiler_params=_params("arbitrary"),
        name="outproj",
    )(oh, of, x, w_out, g_mlp, w_router_t, b_router)


MOE_TILE = 512
MOE_BLK = 256


def _moe_plan_kernel(idx_ref, dest_ref, be_ref, nu_ref, cnt_ref, ps_ref, cnt_scr, run_scr, ps_scr):
    phase = pl.program_id(0)
    step = pl.program_id(1)
    tm = idx_ref.shape[1]
    expert = lax.broadcasted_iota(jnp.int32, (N_EXPERTS, tm), 0)
    idx = idx_ref[...]
    onehot = [expert == idx[j:j + 1, :] for j in range(TOP_K)]

    @pl.when((phase == 0) & (step == 0))
    def _():
        cnt_scr[...] = jnp.zeros_like(cnt_scr)

    @pl.when(phase == 0)
    def _():
        tot = jnp.zeros((N_EXPERTS, 1), F32)
        for oh in onehot:
            tot = tot + jnp.sum(jnp.where(oh, 1.0, 0.0), axis=1, keepdims=True)
        cnt_scr[...] = cnt_scr[...] + tot

    @pl.when((phase == 1) & (step == 0))
    def _():
        cnt = cnt_scr[...]
        padded = jnp.floor((cnt + (MOE_BLK - 1)) * (1.0 / MOE_BLK)) * MOE_BLK
        pstart = _dot01(_tri_lower(N_EXPERTS, strict=True), padded)
        pend = pstart + padded
        ps_scr[...] = pstart
        run_scr[...] = jnp.zeros_like(run_scr)
        cnt_ref[...] = cnt.astype(jnp.int32)
        ps_ref[...] = pstart.astype(jnp.int32)
        nb = be_ref.shape[1]
        row0 = lax.broadcasted_iota(jnp.int32, (1, nb), 1).astype(F32) * MOE_BLK
        be = jnp.sum(jnp.where(pend[:, 0:1] <= row0, 1.0, 0.0), axis=0, keepdims=True)
        be_ref[...] = jnp.minimum(be, N_EXPERTS - 1).astype(jnp.int32)
        total = jnp.max(pend[:, 0:1], axis=0, keepdims=True)
        nu_ref[...] = jnp.broadcast_to(total * (1.0 / MOE_BLK), nu_ref.shape).astype(jnp.int32)

    @pl.when(phase == 1)
    def _():
        r = lax.broadcasted_iota(jnp.int32, (tm, tm), 0)
        c = lax.broadcasted_iota(jnp.int32, (tm, tm), 1)
        before = jnp.where(r < c, 1.0, 0.0).astype(BF16)
        run = run_scr[:, 0:1]
        base = ps_scr[:, 0:1]
        rows = []
        for oh in onehot:
            ohf = jnp.where(oh, 1.0, 0.0)
            prefix = jnp.dot(ohf.astype(BF16), before, preferred_element_type=F32)
            rows.append(jnp.sum(ohf * (base + run + prefix), axis=0, keepdims=True))
            run = run + jnp.sum(ohf, axis=1, keepdims=True)
        run_scr[...] = jnp.broadcast_to(run, run_scr.shape)
        dest_ref[...] = jnp.concatenate(rows, axis=0).astype(jnp.int32)


def _moe_blocks(n_tok):
    return n_tok * TOP_K // MOE_BLK + N_EXPERTS


def _moe_plan(idx_all):
    n_tok = idx_all.shape[1]
    nb = -(-_moe_blocks(n_tok) // LANES) * LANES
    lane_i32 = jax.ShapeDtypeStruct((N_EXPERTS, LANES), jnp.int32)
    const = lambda shape: pl.BlockSpec(shape, lambda p, s: (0, 0))
    dest, be, nu, cnt, ps = pl.pallas_call(
        _moe_plan_kernel,
        grid=(2, n_tok // MOE_TILE),
        in_specs=[pl.BlockSpec((TOP_K, MOE_TILE), lambda p, s: (0, s))],
        out_specs=[pl.BlockSpec((TOP_K, MOE_TILE), lambda p, s: (0, s * p)),
                   const((1, nb)), const((1, LANES)), const((N_EXPERTS, LANES)), const((N_EXPERTS, LANES))],
        out_shape=[jax.ShapeDtypeStruct((TOP_K, n_tok), jnp.int32), jax.ShapeDtypeStruct((1, nb), jnp.int32),
                   jax.ShapeDtypeStruct((1, LANES), jnp.int32), lane_i32, lane_i32],
        scratch_shapes=[pltpu.VMEM((N_EXPERTS, LANES), F32)] * 3,
        compiler_params=_params("arbitrary", "arbitrary"),
        name="moe_plan",
    )(idx_all)
    return dest, be[0], nu[0, :1], cnt[:, 0], ps[:, 0]


def _row_copy(src, dst, sem):
    return pltpu.make_async_copy(src, dst, sem)


def _tile_row(t, n=1):
    return pl.ds(pl.multiple_of(t * SUBLANES, SUBLANES), n * SUBLANES)


def _load_tile_rows(ref, n):
    return jnp.concatenate([ref[pl.ds(c, n, stride=SUBLANES), :] for c in range(SUBLANES)], axis=-1)


def _store_tile_rows(ref, val):
    n = val.shape[0]
    for c in range(SUBLANES):
        ref[pl.ds(c, n, stride=SUBLANES), :] = val[:, c * LANES:(c + 1) * LANES]


def _moe_dispatch_kernel(cnt_ref, ps_ref, dest_ref, up_ref, us_ref, xs_ref, zero_scr, sem, *, n_prompt_tiles):
    i = pl.program_id(0)

    def scatter(src_ref):
        n = src_ref.shape[0] // SUBLANES

        def start(t, carry):
            for j in range(TOP_K):
                _row_copy(src_ref.at[_tile_row(t)], xs_ref.at[_tile_row(dest_ref[j, t])], sem).start()
            return carry

        def wait(t, carry):
            for j in range(TOP_K):
                _row_copy(src_ref.at[_tile_row(0)], xs_ref.at[_tile_row(0)], sem).wait()
            return carry

        lax.fori_loop(0, n, start, 0, unroll=4)
        lax.fori_loop(0, n, wait, 0, unroll=4)

    @pl.when(i < n_prompt_tiles)
    def _():
        scatter(up_ref)

    @pl.when(i == n_prompt_tiles)
    def _():
        scatter(us_ref)

    @pl.when(i == n_prompt_tiles + 1)
    def _():
        zero_scr[...] = jnp.zeros_like(zero_scr)

        def fill(e, start_not_wait):
            cnt = cnt_ref[e]
            pad = (-cnt) & (MOE_BLK - 1)
            off = ps_ref[e] + cnt

            def one(r, carry):
                cp = _row_copy(zero_scr.at[_tile_row(0)], xs_ref.at[_tile_row(off + r)], sem)
                if start_not_wait:
                    cp.start()
                else:
                    cp.wait()
                return carry

            lax.fori_loop(0, pad, one, 0)

        lax.fori_loop(0, N_EXPERTS, lambda e, c: (fill(e, True), c)[1], 0)
        lax.fori_loop(0, N_EXPERTS, lambda e, c: (fill(e, False), c)[1], 0)

        last = N_EXPERTS - 1
        used = (ps_ref[last] + cnt_ref[last] + (MOE_BLK - 1)) // MOE_BLK
        n_blocks = xs_ref.shape[0] // (MOE_BLK * SUBLANES)
        n_zero = zero_scr.shape[0] // SUBLANES

        def tail(start_not_wait):
            def one(b, carry):
                for r in range(0, MOE_BLK, n_zero):
                    cp = _row_copy(zero_scr, xs_ref.at[_tile_row(b * MOE_BLK + r, n_zero)], sem)
                    if start_not_wait:
                        cp.start()
                    else:
                        cp.wait()
                return carry

            lax.fori_loop(used, n_blocks, one, 0)

        tail(True)
        tail(False)


def _moe_dispatch(u_prompt, u_sample, dest, cnt, pstart):
    n_p = u_prompt.shape[0] // SUBLANES
    n_s = u_sample.shape[0] // SUBLANES
    n_tok = dest.shape[1]
    assert n_p % MOE_TILE == 0 and n_s <= MOE_TILE and n_tok == n_p + MOE_TILE
    npt = n_p // MOE_TILE
    cap = _moe_blocks(n_tok) * MOE_BLK
    grid_spec = pltpu.PrefetchScalarGridSpec(
        num_scalar_prefetch=2,
        grid=(npt + 2,),
        in_specs=[pl.BlockSpec((TOP_K, MOE_TILE), lambda i, c, p: (0, jnp.minimum(i, npt)),
                               memory_space=pltpu.SMEM),
                  pl.BlockSpec((MOE_TILE * SUBLANES, LANES), lambda i, c, p: (jnp.minimum(i, npt - 1), 0)),
                  pl.BlockSpec((n_s * SUBLANES, LANES), lambda i, c, p: (0, 0))],
        out_specs=pl.BlockSpec(memory_space=pl.ANY),
        scratch_shapes=[pltpu.VMEM((SUBLANES * SUBLANES, LANES), F32), pltpu.SemaphoreType.DMA(())],
    )
    return pl.pallas_call(
        functools.partial(_moe_dispatch_kernel, n_prompt_tiles=npt),
        grid_spec=grid_spec,
        out_shape=jax.ShapeDtypeStruct((cap * SUBLANES, LANES), F32),
        compiler_params=_params("arbitrary"),
        name="moe_dispatch",
    )(cnt, pstart, dest, u_prompt, u_sample)


def _moe_experts_kernel(be_ref, nu_ref, x_ref, wu_ref, bu_ref, wd_ref, bd_ref, y_ref, wu_bf, wd_bf):
    d_ff = wd_ref.shape[1]
    r = pl.program_id(0)
    active = r < nu_ref[0]
    new_expert = (r == 0) | (be_ref[r] != be_ref[jnp.maximum(r - 1, 0)])

    @pl.when(active & new_expert)
    def _():
        wu_bf[...] = wu_ref[0].astype(BF16)
        wd_bf[...] = wd_ref[0].astype(BF16)

    @pl.when(active)
    def _():
        x = _load_tile_rows(x_ref, MOE_BLK).astype(BF16)
        hdn = jnp.dot(x, wu_bf[...], preferred_element_type=F32) + bu_ref[0]
        glu = jnp.minimum(hdn[:, :d_ff], SWIGLU_LIMIT)
        lin = jnp.clip(hdn[:, d_ff:], -SWIGLU_LIMIT, SWIGLU_LIMIT)
        act = glu * _sigmoid(SWIGLU_ALPHA * glu) * (lin + 1.0)
        _store_tile_rows(y_ref, jnp.dot(act.astype(BF16), wd_bf[...], preferred_element_type=F32) + bd_ref[0])

    @pl.when(pl.program_id(0) >= nu_ref[0])
    def _():
        y_ref[...] = jnp.zeros_like(y_ref)


def _moe_experts(xs, blk_exp, n_used, w_up, b_up, w_down, b_down):
    blk = (MOE_BLK * SUBLANES, LANES)
    nb = xs.shape[0] // blk[0]
    last = lambda r, nu: jnp.minimum(r, nu[0] - 1)
    w_map = lambda r, be, nu: (be[last(r, nu)], 0, 0)
    x_map = lambda r, be, nu: (last(r, nu), 0)
    grid_spec = pltpu.PrefetchScalarGridSpec(
        num_scalar_prefetch=2,
        grid=(nb,),
        in_specs=[pl.BlockSpec(blk, x_map),
                  pl.BlockSpec((1,) + w_up.shape[1:], w_map), pl.BlockSpec((1,) + b_up.shape[1:], w_map),
                  pl.BlockSpec((1,) + w_down.shape[1:], w_map), pl.BlockSpec((1,) + b_down.shape[1:], w_map)],
        out_specs=pl.BlockSpec(blk, lambda r, be, nu: (r, 0)),
        scratch_shapes=[pltpu.VMEM(w_up.shape[1:], BF16), pltpu.VMEM(w_down.shape[1:], BF16)],
    )
    return pl.pallas_call(
        _moe_experts_kernel,
        grid_spec=grid_spec,
        out_shape=jax.ShapeDtypeStruct(xs.shape, F32),
        compiler_params=_params("arbitrary"),
        name="moe_experts",
    )(blk_exp, n_used, xs, w_up, b_up, w_down, b_down)


def _combine_kernel(dest_ref, dest_next_ref, h_ref, p_ref, gate_ref, ys_ref, wg_ref, wp_ref, gf_ref, o_ref,
                    y_scr, sem, *, final):
    tm = h_ref.shape[0]
    i = pl.program_id(0)
    slot = i % 2

    def gather(idx_ref, s):
        def start(t, carry):
            for j in range(TOP_K):
                _row_copy(ys_ref.at[_tile_row(idx_ref[j, t])], y_scr.at[s, j, _tile_row(t)], sem.at[s]).start()
            return carry

        lax.fori_loop(0, tm, start, 0, unroll=4)

    @pl.when(i == 0)
    def _():
        gather(dest_ref, slot)

    @pl.when(i + 1 < pl.num_programs(0))
    def _():
        gather(dest_next_ref, 1 - slot)

    def wait(t, carry):
        for j in range(TOP_K):
            _row_copy(ys_ref.at[_tile_row(0)], y_scr.at[slot, 0, _tile_row(0)], sem.at[slot]).wait()
        return carry

    lax.fori_loop(0, tm, wait, 0, unroll=4)
    gate = gate_ref[...]
    h = h_ref[...]
    cols = []
    for c in range(SUBLANES):
        hc = h[:, c * LANES:(c + 1) * LANES]
        for j in range(TOP_K):
            hc = hc + gate[:, j:j + 1] * y_scr[slot, j, pl.ds(c, tm, stride=SUBLANES), :]
        cols.append(hc)
    h = jnp.concatenate(cols, axis=-1)
    emb_gate = _sigmoid(jnp.dot(h.astype(BF16), wg_ref[...], preferred_element_type=F32))
    emb = jnp.dot(p_ref[...].astype(BF16), wp_ref[...], preferred_element_type=F32)
    h = h + emb_gate * emb
    if final:
        ms = jnp.mean(h * h, axis=-1, keepdims=True)
        h = h * lax.rsqrt(ms + NORM_EPS) * gf_ref[...]
    o_ref[...] = h


def _combine(h1, p_emb, dest, gate_t, ys, w_gate, w_emb, g_final, final):
    m, d = h1.shape
    tm = _row_tile(m, 256)
    full = lambda a: pl.BlockSpec(a.shape, lambda i: (0,) * a.ndim)
    nt = m // tm
    return pl.pallas_call(
        functools.partial(_combine_kernel, final=final),
        grid=(nt,),
        in_specs=[pl.BlockSpec((TOP_K, tm), lambda i: (0, i), memory_space=pltpu.SMEM),
                  pl.BlockSpec((TOP_K, tm), lambda i: (0, jnp.minimum(i + 1, nt - 1)), memory_space=pltpu.SMEM),
                  pl.BlockSpec((tm, d), lambda i: (i, 0)),
                  pl.BlockSpec((tm, p_emb.shape[1]), lambda i: (i, 0)),
                  pl.BlockSpec((tm, TOP_K), lambda i: (i, 0)),
                  pl.BlockSpec(memory_space=pl.ANY),
                  full(w_gate), full(w_emb), full(g_final)],
        out_specs=pl.BlockSpec((tm, d), lambda i: (i, 0)),
        out_shape=jax.ShapeDtypeStruct((m, d), F32),
        scratch_shapes=[pltpu.VMEM((2, TOP_K, tm * SUBLANES, LANES), F32), pltpu.SemaphoreType.DMA((2,))],
        compiler_params=_params("arbitrary"),
        name="combine",
    )(dest, dest, h1, p_emb, gate_t, ys, w_gate, w_emb, g_final)


def kernel(x_prompt, x_sample, state_hgrn, cache_k, cache_v, cache_logf, page_table, p_prompt, p_sample, g_mix, w_in, hg_lb_logits, g_hg_out, fox_f_bias, g_fox_out, w_out, g_mlp, w_router, b_router, w_up, b_up, w_down, b_down, w_ple, w_ple_gate, g_final):
    depth = w_in.shape[0]
    bsz, seq, d = x_prompt.shape
    dec_b = x_sample.shape[0]
    n_p, n_s = bsz * seq, dec_b
    n_phys = cache_k.shape[1]
    lb_all = jnp.cumsum(jax.nn.softmax(hg_lb_logits.astype(F32), axis=0), axis=0)
    hp = x_prompt.reshape(n_p, d)
    hs = x_sample.reshape(n_s, d)
    n_main = 7 * HG_WIDTH
    outs = [[] for _ in range(8)]
    for i in range(depth):
        final = i == depth - 1
        w_main = w_in[i, :, :n_main].astype(BF16)
        w_ff = jnp.pad(w_in[i, :, n_main:], ((0, 0), (0, LANES - FOX_HEADS))).astype(BF16)
        f_bias = jnp.pad(fox_f_bias[i], (0, LANES - FOX_HEADS)).reshape(1, LANES)
        g_mix_i = g_mix[i].reshape(1, d)
        lb_i = lb_all[i].reshape(1, HG_WIDTH)
        g_hg = g_hg_out[i].reshape(1, HG_WIDTH)
        g_fox = g_fox_out[i].reshape(1, FOX_WIDTH)
        w_out_i = w_out[i].astype(BF16)
        g_mlp_i = g_mlp[i].reshape(1, d)
        w_r_t = w_router[i].T
        b_r = b_router[i].reshape(N_EXPERTS, 1)

        q, f, v, gt, fq, fk, fv, lf, kb, vb, c = _inproj(hp, g_mix_i, w_main, w_ff, lb_i, f_bias, seq)
        oh, sp = _hgrn_prompt(q, f, v, gt, g_hg, bsz, seq)
        of = _fox_prompt(fq, kb, vb, c, c.T, g_fox, bsz, seq)
        h1_p, u_p, idx_p, gate_p = _outproj(oh, of, hp, w_out_i, g_mlp_i, w_r_t, b_r)
        outs[0].append(sp)
        outs[1].append(fk.reshape(bsz, seq, FOX_HEADS, FOX_DH))
        outs[2].append(fv.reshape(bsz, seq, FOX_HEADS, FOX_DH))
        outs[3].append(lf.reshape(bsz, seq, FOX_HEADS))

        q, f, v, gt, fq, fk, fv, lf = _inproj(hs, g_mix_i, w_main, w_ff, lb_i, f_bias, 1)
        oh, ss = _hgrn_step(q, f, v, gt, g_hg, state_hgrn[i])
        kv_t = lambda a: jnp.transpose(a[i], (0, 2, 3, 1)).reshape(n_phys, FOX_WIDTH, PAGE_SIZE)
        of = _fox_decode(fq, fk, fv, lf, g_fox, kv_t(cache_k), kv_t(cache_v),
                         jnp.transpose(cache_logf[i], (0, 2, 1)), page_table)
        h1_s, u_s, idx_s, gate_s = _outproj(oh, of, hs, w_out_i, g_mlp_i, w_r_t, b_r)
        outs[4].append(ss)
        outs[5].append(fk.reshape(dec_b, 1, FOX_HEADS, FOX_DH))
        outs[6].append(fv.reshape(dec_b, 1, FOX_HEADS, FOX_DH))
        outs[7].append(lf.reshape(dec_b, 1, FOX_HEADS))

        n_tile = -(-(n_p + n_s) // MOE_TILE) * MOE_TILE
        idx_all = jnp.concatenate(
            [idx_p, idx_s, jnp.full((TOP_K, n_tile - n_p - n_s), N_EXPERTS, jnp.int32)], axis=1)
        dest, blk_exp, n_used, cnt, pstart = _moe_plan(idx_all)
        xs = _moe_dispatch(u_p, u_s, dest, cnt, pstart)
        ys = _moe_experts(xs, blk_exp, n_used, w_up[i], b_up[i].reshape(N_EXPERTS, 1, -1),
                          w_down[i], b_down[i].reshape(N_EXPERTS, 1, -1))
        w_pg = w_ple_gate[i].astype(BF16)
        w_p = w_ple[i].astype(BF16)
        g_fin = g_final.reshape(1, d)
        hp = _combine(h1_p, p_prompt[i].reshape(n_p, -1), dest[:, :n_p], gate_p.T, ys, w_pg, w_p, g_fin, final)
        hs = _combine(h1_s, p_sample[i].reshape(n_s, -1), dest[:, n_p:n_p + n_s], gate_s.T, ys, w_pg, w_p, g_fin,
                      final)
    st = [jnp.stack(o) for o in outs]
    return (hp.reshape(bsz, seq, d), hs.reshape(dec_b, 1, d), *st)
```

```python
import functools

import jax
import jax.numpy as jnp
from jax import lax
from jax.experimental import pallas as pl
from jax.experimental.pallas import tpu as pltpu

F32 = jnp.float32
BF16 = jnp.bfloat16

HG_HEADS = 4
HG_DK = 128
HG_DV = 128
HG_WIDTH = HG_HEADS * HG_DK
FOX_HEADS = 8
FOX_DH = 64
FOX_WIDTH = FOX_HEADS * FOX_DH
N_EXPERTS = 32
TOP_K = 4
PAGE_SIZE = 128
NORM_EPS = 1e-6
SWIGLU_ALPHA = 1.702
SWIGLU_LIMIT = 7.0

LANES = 128
SUBLANES = 8
NEG_BIG = -1e30
VMEM_LIMIT = 56 * 1024 * 1024

HG_CHUNK = 16
HG_BLOCK = 128
FOX_TQ = 512
FOX_TK = 512

def _params(*sem):
    return pltpu.CompilerParams(dimension_semantics=sem, vmem_limit_bytes=VMEM_LIMIT)


def _row_tile(m, pref):
    return pref if m % pref == 0 else m


def _split3(x):
    hi = x.astype(BF16)
    r1 = x - hi.astype(F32)
    mid = r1.astype(BF16)
    lo = (r1 - mid.astype(F32)).astype(BF16)
    return hi, mid, lo


def _dot01(mask_bf16, x):
    hi, mid, lo = _split3(x)
    d = lambda a: jnp.dot(mask_bf16, a, preferred_element_type=F32)
    return d(hi) + d(mid) + d(lo)


def _tri_lower(n, strict=False):
    r = lax.broadcasted_iota(jnp.int32, (n, n), 0)
    c = lax.broadcasted_iota(jnp.int32, (n, n), 1)
    m = (c < r) if strict else (c <= r)
    return jnp.where(m, 1.0, 0.0).astype(BF16)


def _sigmoid(x):
    return 1.0 / (1.0 + jnp.exp(-x))


def _silu(x):
    return x * _sigmoid(x)


def _log_sigmoid(x):
    return jnp.minimum(x, 0.0) - jnp.log1p(jnp.exp(-jnp.abs(x)))


def _inproj_kernel(x_ref, g_ref, w_ref, wf_ref, lb_ref, fb_ref,
                   q_ref, f_ref, i_ref, gt_ref, fq_ref, fk_ref, fv_ref, lf_ref, *rest, tiles_per_seq):
    x = x_ref[...]
    ms = jnp.mean(x * x, axis=-1, keepdims=True)
    u = (x * lax.rsqrt(ms + NORM_EPS) * g_ref[...]).astype(BF16)

    def proj(j):
        return jnp.dot(u, w_ref[:, j * HG_WIDTH:(j + 1) * HG_WIDTH], preferred_element_type=F32)

    q_ref[...] = _silu(proj(0))
    lb = lb_ref[...]
    f_ref[...] = lb + (1.0 - lb) * _sigmoid(proj(1))
    i_ref[...] = proj(2)
    gt_ref[...] = _silu(proj(3))
    fq_ref[...] = (proj(4) * (FOX_DH ** -0.5)).astype(BF16)
    fk = proj(5)
    fk_ref[...] = fk
    fv = proj(6)
    fv_ref[...] = fv
    ff = jnp.dot(u, wf_ref[...], preferred_element_type=F32) + fb_ref[...]
    lf = _log_sigmoid(ff)
    lf_ref[...] = lf[:, :FOX_HEADS]
    if tiles_per_seq:
        kb_ref, vb_ref, c_ref, carry_ref = rest
        kb_ref[...] = fk.astype(BF16)
        vb_ref[...] = fv.astype(BF16)
        tm = x.shape[0]

        @pl.when(pl.program_id(0) % tiles_per_seq == 0)
        def _():
            carry_ref[...] = jnp.zeros_like(carry_ref)

        c = _dot01(_tri_lower(tm), lf) + carry_ref[0:1, :]
        carry_ref[...] = jnp.broadcast_to(c[tm - 1:tm, :], carry_ref.shape)
        c_ref[...] = c[:, :FOX_HEADS]


def _inproj(x2d, g_mix, w_main, w_ff, lb, f_bias, seq_len):
    m, d = x2d.shape
    tm = _row_tile(m if seq_len == 1 else seq_len, 512)
    tiles_per_seq = 0 if seq_len == 1 else seq_len // tm
    row = lambda w: pl.BlockSpec((tm, w), lambda i: (i, 0))
    full = lambda a: pl.BlockSpec(a.shape, lambda i: (0,) * a.ndim)
    wide = jax.ShapeDtypeStruct((m, HG_WIDTH), F32)
    wide_bf = jax.ShapeDtypeStruct((m, HG_WIDTH), BF16)
    narrow = jax.ShapeDtypeStruct((m, FOX_HEADS), F32)
    out_specs = [row(HG_WIDTH)] * 7 + [row(FOX_HEADS)]
    out_shape = [wide, wide, wide, wide, wide_bf, wide, wide, narrow]
    scratch = []
    if tiles_per_seq:
        out_specs += [row(HG_WIDTH), row(HG_WIDTH), row(FOX_HEADS)]
        out_shape += [wide_bf, wide_bf, narrow]
        scratch = [pltpu.VMEM((SUBLANES, LANES), F32)]
    return pl.pallas_call(
        functools.partial(_inproj_kernel, tiles_per_seq=tiles_per_seq),
        grid=(m // tm,),
        in_specs=[row(d), full(g_mix), full(w_main), full(w_ff), full(lb), full(f_bias)],
        out_specs=out_specs,
        out_shape=out_shape,
        scratch_shapes=scratch,
        compiler_params=_params("arbitrary"),
        name="inproj",
    )(x2d, g_mix, w_main, w_ff, lb, f_bias)


def _hgrn_kernel(q_ref, f_ref, v_ref, gt_ref, gn_ref, o_ref, s_ref, s_scr, b_scr):
    tb = q_ref.shape[0]

    @pl.when(pl.program_id(1) == 0)
    def _():
        s_scr[...] = jnp.zeros_like(s_scr)

    tri = _tri_lower(HG_BLOCK)
    row_id = lax.broadcasted_iota(jnp.int32, (HG_CHUNK, 1), 0)
    ones_kk = jnp.ones((HG_DK, HG_DK), BF16)

    def block_body(blk, carry):
        r_blk = pl.multiple_of(blk * HG_BLOCK, HG_BLOCK)
        b_scr[...] = _dot01(tri, jnp.log(f_ref[pl.ds(r_blk, HG_BLOCK), :]))

        def chunk_body(ci, carry2):
            r_loc = pl.multiple_of(ci * HG_CHUNK, HG_CHUNK)
            r0 = r_blk + r_loc
            for h in range(HG_HEADS):
                cols = slice(h * HG_DK, (h + 1) * HG_DK)
                q = q_ref[pl.ds(r0, HG_CHUNK), cols]
                k = 1.0 - f_ref[pl.ds(r0, HG_CHUNK), cols]
                v = v_ref[pl.ds(r0, HG_CHUNK), cols]
                bb = b_scr[pl.ds(r_loc, HG_CHUNK), cols]
                b_prev = bb[0:1, :] - jnp.log(f_ref[pl.ds(r0, 1), cols])
                b = bb - b_prev
                st_old = s_scr[h]
                o = lax.dot_general((q * jnp.exp(b)).astype(BF16), st_old.astype(BF16), (((1,), (1,)), ((), ())),
                                    preferred_element_type=F32)
                terms, first_row = [], []
                for s in range(HG_CHUNK):
                    t0 = (s // SUBLANES) * SUBLANES
                    d = jnp.where(row_id[t0:] >= s, b[t0:] - b[s:s + 1, :], NEG_BIG)
                    terms.append(jnp.exp(d) * q[t0:] * k[s:s + 1, :])
                    first_row.append(t0)
                a_all = jnp.dot(jnp.concatenate(terms, axis=0).astype(BF16), ones_kk, preferred_element_type=F32)
                o_rows = [o[t:t + SUBLANES] for t in range(0, HG_CHUNK, SUBLANES)]
                at = 0
                for s in range(HG_CHUNK):
                    for g in range(first_row[s] // SUBLANES, HG_CHUNK // SUBLANES):
                        o_rows[g] = o_rows[g] + a_all[at:at + SUBLANES] * v[s:s + 1, :]
                        at += SUBLANES
                o = jnp.concatenate(o_rows, axis=0)
                b_last = b[HG_CHUNK - 1:HG_CHUNK, :]
                kd = (k * jnp.exp(b_last - b)).astype(BF16)
                upd_t = lax.dot_general(v.astype(BF16), kd, (((0,), (0,)), ((), ())),
                                        preferred_element_type=F32)
                s_scr[h] = st_old * jnp.exp(b_last) + upd_t
                ms = jnp.mean(o * o, axis=-1, keepdims=True)
                o = o * lax.rsqrt(ms + NORM_EPS) * gn_ref[:, cols] * gt_ref[pl.ds(r0, HG_CHUNK), cols]
                o_ref[pl.ds(r0, HG_CHUNK), cols] = o.astype(BF16)
            return carry2

        return lax.fori_loop(0, HG_BLOCK // HG_CHUNK, chunk_body, carry, unroll=2)

    lax.fori_loop(0, tb // HG_BLOCK, block_body, 0)

    @pl.when(pl.program_id(1) == pl.num_programs(1) - 1)
    def _():
        for h in range(HG_HEADS):
            s_ref[0, h] = jnp.transpose(s_scr[h])


def _hgrn_prompt(q, f, v, gt, g_norm, batch, seq):
    tb = _row_tile(seq, 512)
    assert tb % HG_BLOCK == 0
    nt = seq // tb
    row = pl.BlockSpec((tb, HG_WIDTH), lambda b, t: (b * nt + t, 0))
    return pl.pallas_call(
        _hgrn_kernel,
        grid=(batch, nt),
        in_specs=[row, row, row, row, pl.BlockSpec((1, HG_WIDTH), lambda b, t: (0, 0))],
        out_specs=[row, pl.BlockSpec((1, HG_HEADS, HG_DK, HG_DV), lambda b, t: (b, 0, 0, 0))],
        out_shape=[jax.ShapeDtypeStruct((batch * seq, HG_WIDTH), BF16),
                   jax.ShapeDtypeStruct((batch, HG_HEADS, HG_DK, HG_DV), F32)],
        scratch_shapes=[pltpu.VMEM((HG_HEADS, HG_DK, HG_DV), F32), pltpu.VMEM((HG_BLOCK, HG_WIDTH), F32)],
        compiler_params=_params("arbitrary", "arbitrary"),
        name="hgrn_prompt",
    )(q, f, v, gt, g_norm)


def _fox_prompt_kernel(q_ref, k_ref, v_ref, cq_ref, ck_ref, gn_ref, o_ref, m_scr, l_scr, acc_scr):
    pair = pl.program_id(1)
    qi = pl.program_id(2)
    ki = pl.program_id(3)
    tq = q_ref.shape[0]
    tk = k_ref.shape[0]

    @pl.when(ki == 0)
    def _():
        m_scr[...] = jnp.full_like(m_scr, NEG_BIG)
        l_scr[...] = jnp.zeros_like(l_scr)
        acc_scr[...] = jnp.zeros_like(acc_scr)

    last_k = ((qi + 1) * tq - 1) // tk

    @pl.when(ki <= last_k)
    def _():
        lane = lax.broadcasted_iota(jnp.int32, (1, LANES), 1)
        first = lane < FOX_DH
        q = q_ref[...]
        k = k_ref[...]
        v = v_ref[...]
        q_pos = qi * tq + lax.broadcasted_iota(jnp.int32, (tq, tk), 0)
        k_pos = ki * tk + lax.broadcasted_iota(jnp.int32, (tq, tk), 1)
        causal = k_pos <= q_pos
        cq_all = cq_ref[...]
        ck_all = ck_ref[...]
        head_col = lax.broadcasted_iota(jnp.int32, (1, FOX_HEADS), 1)
        head_row = lax.broadcasted_iota(jnp.int32, (FOX_HEADS, 1), 0)
        pv = []
        for j in range(2):
            h = 2 * pair + j
            qh = jnp.where(first if j == 0 else ~first, q, jnp.zeros_like(q))
            s = lax.dot_general(qh, k, (((1,), (1,)), ((), ())), preferred_element_type=F32)
            cq = jnp.sum(jnp.where(head_col == h, cq_all, 0.0), axis=1, keepdims=True)
            ck = jnp.sum(jnp.where(head_row == h, ck_all, 0.0), axis=0, keepdims=True)
            s = jnp.where(causal, s + cq - ck, NEG_BIG)
            m_old = m_scr[j]
            m_new = jnp.maximum(m_old, jnp.max(s, axis=-1, keepdims=True))
            alpha = jnp.exp(m_old - m_new)
            p = jnp.exp(s - m_new)
            l_scr[j] = alpha * l_scr[j] + jnp.sum(p, axis=-1, keepdims=True)
            m_scr[j] = m_new
            pv.append((alpha, jnp.dot(p.astype(BF16), v, preferred_element_type=F32)))
        alpha = jnp.where(first, pv[0][0], pv[1][0])
        acc_scr[...] = alpha * acc_scr[...] + jnp.where(first, pv[0][1], pv[1][1])

    @pl.when(ki == last_k)
    def _():
        lane = lax.broadcasted_iota(jnp.int32, (1, LANES), 1)
        first = lane < FOX_DH
        o = acc_scr[...] / jnp.where(first, l_scr[0], l_scr[1])
        sq = o * o
        ss0 = jnp.sum(jnp.where(first, sq, 0.0), axis=-1, keepdims=True)
        ss1 = jnp.sum(jnp.where(first, 0.0, sq), axis=-1, keepdims=True)
        ms = jnp.where(first, ss0, ss1) * (1.0 / FOX_DH)
        o_ref[...] = (o * lax.rsqrt(ms + NORM_EPS) * gn_ref[...]).astype(BF16)


def _fox_prompt(q_bf, k_bf, v_bf, c_col, c_row, g_norm, batch, seq):
    tq = _row_tile(seq, FOX_TQ)
    tk = _row_tile(seq, FOX_TK)
    nq, nk = seq // tq, seq // tk
    pairs = FOX_HEADS // 2
    k_blk = lambda i, j: jnp.minimum(j, ((i + 1) * tq - 1) // tk)
    kv_map = lambda b, p, i, j: (b * nk + k_blk(i, j), p)
    return pl.pallas_call(
        _fox_prompt_kernel,
        grid=(batch, pairs, nq, nk),
        in_specs=[pl.BlockSpec((tq, LANES), lambda b, p, i, j: (b * nq + i, p)),
                  pl.BlockSpec((tk, LANES), kv_map),
                  pl.BlockSpec((tk, LANES), kv_map),
                  pl.BlockSpec((tq, FOX_HEADS), lambda b, p, i, j: (b * nq + i, 0)),
                  pl.BlockSpec((FOX_HEADS, tk), lambda b, p, i, j: (0, b * nk + k_blk(i, j))),
                  pl.BlockSpec((1, LANES), lambda b, p, i, j: (0, p))],
        out_specs=pl.BlockSpec((tq, LANES), lambda b, p, i, j: (b * nq + i, p)),
        out_shape=jax.ShapeDtypeStruct((batch * seq, FOX_WIDTH), BF16),
        scratch_shapes=[pltpu.VMEM((2, tq, 1), F32), pltpu.VMEM((2, tq, 1), F32),
                        pltpu.VMEM((tq, LANES), F32)],
        compiler_params=_params("arbitrary", "arbitrary", "arbitrary", "arbitrary"),
        name="fox_prompt",
    )(q_bf, k_bf, v_bf, c_col, c_row, g_norm)


def _hgrn_step_kernel(q_ref, f_ref, v_ref, gt_ref, gn_ref, s_ref, o_ref, so_ref):
    rows = q_ref.shape[0]

    def col(row):
        return jnp.transpose(jnp.broadcast_to(row, (HG_DK, HG_DK)))

    for r in range(rows):
        for h in range(HG_HEADS):
            cols = slice(h * HG_DK, (h + 1) * HG_DK)
            f_col = col(f_ref[r:r + 1, cols])
            q_col = col(q_ref[r:r + 1, cols])
            s_new = f_col * s_ref[r, h] + (1.0 - f_col) * v_ref[r:r + 1, cols]
            so_ref[r, h] = s_new
            o = jnp.sum(s_new * q_col, axis=0, keepdims=True)
            ms = jnp.mean(o * o, axis=-1, keepdims=True)
            o = o * lax.rsqrt(ms + NORM_EPS) * gn_ref[:, cols] * gt_ref[r:r + 1, cols]
            o_ref[r:r + 1, cols] = o.astype(BF16)


def _hgrn_step(q, f, v, gt, g_norm, state):
    n = q.shape[0]
    rows = SUBLANES
    assert n % rows == 0
    row = pl.BlockSpec((rows, HG_WIDTH), lambda i: (i, 0))
    st = pl.BlockSpec((rows, HG_HEADS, HG_DK, HG_DV), lambda i: (i, 0, 0, 0))
    return pl.pallas_call(
        _hgrn_step_kernel,
        grid=(n // rows,),
        in_specs=[row, row, row, row, pl.BlockSpec((1, HG_WIDTH), lambda i: (0, 0)), st],
        out_specs=[row, st],
        out_shape=[jax.ShapeDtypeStruct((n, HG_WIDTH), BF16), jax.ShapeDtypeStruct(state.shape, F32)],
        compiler_params=_params("arbitrary"),
        name="hgrn_step",
    )(q, f, v, gt, g_norm, state)


FOX_PAGES_PER_STEP = 16


def _dot01_rhs(x, mask_bf16):
    hi, mid, lo = _split3(x)
    d = lambda a: jnp.dot(a, mask_bf16, preferred_element_type=F32)
    return d(hi) + d(mid) + d(lo)


def _fox_decode_kernel(pt_ref, q_ref, kn_ref, vn_ref, lfn_ref, gn_ref, *refs):
    del pt_ref
    npg = FOX_PAGES_PER_STEP
    k_refs, v_refs, lf_refs = refs[:npg], refs[npg:2 * npg], refs[2 * npg:3 * npg]
    o_ref, m_scr, l_scr, r_scr, acc_scr = refs[3 * npg:]
    c = pl.program_id(1)
    head = lax.broadcasted_iota(jnp.int32, (FOX_HEADS, FOX_WIDTH), 0)
    lane = lax.broadcasted_iota(jnp.int32, (FOX_HEADS, FOX_WIDTH), 1)
    own = lane // FOX_DH == head
    q_rows = jnp.where(own, q_ref[0].astype(F32), 0.0)
    lf_new = lfn_ref[0]

    @pl.when(c == 0)
    def _():
        k_new = kn_ref[0].astype(BF16).astype(F32)
        m_scr[...] = jnp.sum(q_rows * k_new, axis=1, keepdims=True)
        l_scr[...] = jnp.ones_like(l_scr)
        r_scr[...] = jnp.zeros_like(r_scr)
        acc_scr[...] = jnp.broadcast_to(vn_ref[0], acc_scr.shape)

    q_bf = q_rows.astype(BF16)
    rr = lax.broadcasted_iota(jnp.int32, (PAGE_SIZE, PAGE_SIZE), 0)
    cc = lax.broadcasted_iota(jnp.int32, (PAGE_SIZE, PAGE_SIZE), 1)
    later = jnp.where(rr > cc, 1.0, 0.0).astype(BF16)

    def stack(x):
        return jnp.concatenate([x[:, g * PAGE_SIZE:(g + 1) * PAGE_SIZE] for g in range(npg)], axis=0)

    def unstack(x):
        return jnp.concatenate([x[g * FOX_HEADS:(g + 1) * FOX_HEADS] for g in range(npg)], axis=1)

    def per_head(x, op):
        out = x[0:FOX_HEADS]
        for g in range(1, npg):
            out = op(out, x[g * FOX_HEADS:(g + 1) * FOX_HEADS])
        return out

    k_cat = jnp.concatenate([k_refs[g][0].astype(BF16) for g in range(npg)], axis=1)
    v_cat = jnp.concatenate([v_refs[g][0].astype(BF16) for g in range(npg)], axis=1)
    lf = jnp.concatenate([lf_refs[g][0] for g in range(npg)], axis=0)
    page_mass = jnp.sum(lf, axis=1, keepdims=True)
    run = r_scr[...]
    before = []
    for g in range(npg):
        before.append(run)
        run = run + page_mass[g * FOX_HEADS:(g + 1) * FOX_HEADS]
    r_scr[...] = run
    bias = _dot01_rhs(lf, later) + jnp.concatenate(before, axis=0) + jnp.concatenate([lf_new] * npg, axis=0)
    s = stack(jnp.dot(q_bf, k_cat, preferred_element_type=F32)) + bias

    m_old = m_scr[...]
    m_new = jnp.maximum(m_old, per_head(jnp.max(s, axis=1, keepdims=True), jnp.maximum))
    alpha = jnp.exp(m_old - m_new)
    p = jnp.exp(s - jnp.concatenate([m_new] * npg, axis=0))
    l_scr[...] = alpha * l_scr[...] + per_head(jnp.sum(p, axis=1, keepdims=True), jnp.add)
    m_scr[...] = m_new
    pv = lax.dot_general(unstack(p).astype(BF16), v_cat, (((1,), (1,)), ((), ())),
                         preferred_element_type=F32)
    acc_scr[...] = alpha * acc_scr[...] + pv

    @pl.when(c == pl.num_programs(1) - 1)
    def _():
        o = jnp.where(own, acc_scr[...] / l_scr[...], 0.0)
        ms = jnp.sum(o * o, axis=1, keepdims=True) * (1.0 / FOX_DH)
        o = o * lax.rsqrt(ms + NORM_EPS)
        o_ref[0] = (jnp.sum(o, axis=0, keepdims=True) * gn_ref[...]).astype(BF16)


def _fox_decode(q_bf, k_new, v_new, lf_new, g_norm, cache_kt, cache_vt, cache_lft, page_table):
    n, n_pages = page_table.shape
    npg = FOX_PAGES_PER_STEP
    assert n_pages % npg == 0
    n_chunks = n_pages // npg

    def page_map(g):
        return lambda b, c, pt: (pt[b, (n_chunks - 1 - c) * npg + (npg - 1 - g)], 0, 0)

    row3 = lambda h, w: pl.BlockSpec((1, h, w), lambda b, c, pt: (b, 0, 0))
    kv_specs = [pl.BlockSpec((1, FOX_WIDTH, PAGE_SIZE), page_map(g)) for g in range(npg)]
    lf_specs = [pl.BlockSpec((1, FOX_HEADS, PAGE_SIZE), page_map(g)) for g in range(npg)]
    stat = pltpu.VMEM((FOX_HEADS, 1), F32)
    grid_spec = pltpu.PrefetchScalarGridSpec(
        num_scalar_prefetch=1,
        grid=(n, n_chunks),
        in_specs=[row3(1, FOX_WIDTH), row3(1, FOX_WIDTH), row3(1, FOX_WIDTH), row3(FOX_HEADS, 1),
                  pl.BlockSpec((1, FOX_WIDTH), lambda b, c, pt: (0, 0))] + kv_specs + kv_specs + lf_specs,
        out_specs=row3(1, FOX_WIDTH),
        scratch_shapes=[stat, stat, stat, pltpu.VMEM((FOX_HEADS, FOX_WIDTH), F32)],
    )
    out = pl.pallas_call(
        _fox_decode_kernel,
        grid_spec=grid_spec,
        out_shape=jax.ShapeDtypeStruct((n, 1, FOX_WIDTH), BF16),
        compiler_params=_params("arbitrary", "arbitrary"),
        name="fox_decode",
    )(page_table, q_bf.reshape(n, 1, FOX_WIDTH), k_new.reshape(n, 1, FOX_WIDTH), v_new.reshape(n, 1, FOX_WIDTH),
      lf_new.reshape(n, FOX_HEADS, 1), g_norm, *([cache_kt] * npg), *([cache_vt] * npg), *([cache_lft] * npg))
    return out.reshape(n, FOX_WIDTH)


def _outproj_kernel(oh_ref, of_ref, x_ref, w_ref, g_ref, wr_ref, br_ref, h_ref, u_ref, idx_ref, gate_ref):
    mix_in = jnp.concatenate([oh_ref[...], of_ref[...]], axis=-1)
    h = x_ref[...] + jnp.dot(mix_in, w_ref[...], preferred_element_type=F32)
    h_ref[...] = h
    ms = jnp.mean(h * h, axis=-1, keepdims=True)
    u = h * lax.rsqrt(ms + NORM_EPS) * g_ref[...]
    _store_tile_rows(u_ref, u)
    vals = lax.dot_general(wr_ref[...], u, (((1,), (1,)), ((), ())), preferred_element_type=F32,
                           precision=lax.Precision.HIGHEST) + br_ref[...]
    expert = lax.broadcasted_iota(jnp.int32, vals.shape, 0)
    top_v, top_i = [], []
    for _ in range(TOP_K):
        m = jnp.max(vals, axis=0, keepdims=True)
        sel = jnp.min(jnp.where(vals == m, expert, N_EXPERTS), axis=0, keepdims=True)
        vals = jnp.where(expert == sel, -jnp.inf, vals)
        top_v.append(m)
        top_i.append(sel)
    ex = [jnp.exp(t - top_v[0]) for t in top_v]
    den = ex[0] + ex[1] + ex[2] + ex[3]
    idx_ref[...] = jnp.concatenate(top_i, axis=0)
    gate_ref[...] = jnp.concatenate([x / den for x in ex], axis=0)


def _outproj(oh, of, x, w_out, g_mlp, w_router_t, b_router):
    m, d = x.shape
    assert d == SUBLANES * LANES
    tm = _row_tile(m, 512)
    row = lambda w: pl.BlockSpec((tm, w), lambda i: (i, 0))
    full = lambda a: pl.BlockSpec(a.shape, lambda i: (0,) * a.ndim)
    colblk = pl.BlockSpec((TOP_K, tm), lambda i: (0, i))
    return pl.pallas_call(
        _outproj_kernel,
        grid=(m // tm,),
        in_specs=[row(HG_WIDTH), row(FOX_WIDTH), row(d), full(w_out), full(g_mlp), full(w_router_t),
                  full(b_router)],
        out_specs=[row(d), pl.BlockSpec((tm * SUBLANES, LANES), lambda i: (i, 0)), colblk, colblk],
        out_shape=[jax.ShapeDtypeStruct((m, d), F32), jax.ShapeDtypeStruct((m * SUBLANES, LANES), F32),
                   jax.ShapeDtypeStruct((TOP_K, m), jnp.int32), jax.ShapeDtypeStruct((TOP_K, m), F32)],
        compiler_params=_params("arbitrary"),
        name="outproj",
    )(oh, of, x, w_out, g_mlp, w_router_t, b_router)


MOE_TILE = 512
MOE_BLK = 256


def _moe_plan_kernel(idx_ref, dest_ref, be_ref, nu_ref, cnt_ref, ps_ref, cnt_scr, run_scr, ps_scr):
    phase = pl.program_id(0)
    step = pl.program_id(1)
    tm = idx_ref.shape[1]
    expert = lax.broadcasted_iota(jnp.int32, (N_EXPERTS, tm), 0)
    idx = idx_ref[...]
    onehot = [expert == idx[j:j + 1, :] for j in range(TOP_K)]

    @pl.when((phase == 0) & (step == 0))
    def _():
        cnt_scr[...] = jnp.zeros_like(cnt_scr)

    @pl.when(phase == 0)
    def _():
        tot = jnp.zeros((N_EXPERTS, 1), F32)
        for oh in onehot:
            tot = tot + jnp.sum(jnp.where(oh, 1.0, 0.0), axis=1, keepdims=True)
        cnt_scr[...] = cnt_scr[...] + tot

    @pl.when((phase == 1) & (step == 0))
    def _():
        cnt = cnt_scr[...]
        padded = jnp.floor((cnt + (MOE_BLK - 1)) * (1.0 / MOE_BLK)) * MOE_BLK
        pstart = _dot01(_tri_lower(N_EXPERTS, strict=True), padded)
        pend = pstart + padded
        ps_scr[...] = pstart
        run_scr[...] = jnp.zeros_like(run_scr)
        cnt_ref[...] = cnt.astype(jnp.int32)
        ps_ref[...] = pstart.astype(jnp.int32)
        nb = be_ref.shape[1]
        row0 = lax.broadcasted_iota(jnp.int32, (1, nb), 1).astype(F32) * MOE_BLK
        be = jnp.sum(jnp.where(pend[:, 0:1] <= row0, 1.0, 0.0), axis=0, keepdims=True)
        be_ref[...] = jnp.minimum(be, N_EXPERTS - 1).astype(jnp.int32)
        total = jnp.max(pend[:, 0:1], axis=0, keepdims=True)
        nu_ref[...] = jnp.broadcast_to(total * (1.0 / MOE_BLK), nu_ref.shape).astype(jnp.int32)

    @pl.when(phase == 1)
    def _():
        r = lax.broadcasted_iota(jnp.int32, (tm, tm), 0)
        c = lax.broadcasted_iota(jnp.int32, (tm, tm), 1)
        before = jnp.where(r < c, 1.0, 0.0).astype(BF16)
        run = run_scr[:, 0:1]
        base = ps_scr[:, 0:1]
        rows = []
        for oh in onehot:
            ohf = jnp.where(oh, 1.0, 0.0)
            prefix = jnp.dot(ohf.astype(BF16), before, preferred_element_type=F32)
            rows.append(jnp.sum(ohf * (base + run + prefix), axis=0, keepdims=True))
            run = run + jnp.sum(ohf, axis=1, keepdims=True)
        run_scr[...] = jnp.broadcast_to(run, run_scr.shape)
        dest_ref[...] = jnp.concatenate(rows, axis=0).astype(jnp.int32)


def _moe_blocks(n_tok):
    return n_tok * TOP_K // MOE_BLK + N_EXPERTS


def _moe_plan(idx_all):
    n_tok = idx_all.shape[1]
    nb = -(-_moe_blocks(n_tok) // LANES) * LANES
    lane_i32 = jax.ShapeDtypeStruct((N_EXPERTS, LANES), jnp.int32)
    const = lambda shape: pl.BlockSpec(shape, lambda p, s: (0, 0))
    dest, be, nu, cnt, ps = pl.pallas_call(
        _moe_plan_kernel,
        grid=(2, n_tok // MOE_TILE),
        in_specs=[pl.BlockSpec((TOP_K, MOE_TILE), lambda p, s: (0, s))],
        out_specs=[pl.BlockSpec((TOP_K, MOE_TILE), lambda p, s: (0, s * p)),
                   const((1, nb)), const((1, LANES)), const((N_EXPERTS, LANES)), const((N_EXPERTS, LANES))],
        out_shape=[jax.ShapeDtypeStruct((TOP_K, n_tok), jnp.int32), jax.ShapeDtypeStruct((1, nb), jnp.int32),
                   jax.ShapeDtypeStruct((1, LANES), jnp.int32), lane_i32, lane_i32],
        scratch_shapes=[pltpu.VMEM((N_EXPERTS, LANES), F32)] * 3,
        compiler_params=_params("arbitrary", "arbitrary"),
        name="moe_plan",
    )(idx_all)
    return dest, be[0], nu[0, :1], cnt[:, 0], ps[:, 0]


def _row_copy(src, dst, sem):
    return pltpu.make_async_copy(src, dst, sem)


def _tile_row(t, n=1):
    return pl.ds(pl.multiple_of(t * SUBLANES, SUBLANES), n * SUBLANES)


def _load_tile_rows(ref, n):
    return jnp.concatenate([ref[pl.ds(c, n, stride=SUBLANES), :] for c in range(SUBLANES)], axis=-1)


def _store_tile_rows(ref, val):
    n = val.shape[0]
    for c in range(SUBLANES):
        ref[pl.ds(c, n, stride=SUBLANES), :] = val[:, c * LANES:(c + 1) * LANES]


def _moe_dispatch_kernel(cnt_ref, ps_ref, dest_ref, up_ref, us_ref, xs_ref, zero_scr, sem, *, n_prompt_tiles):
    i = pl.program_id(0)

    def scatter(src_ref):
        n = src_ref.shape[0] // SUBLANES

        def start(t, carry):
            for j in range(TOP_K):
                _row_copy(src_ref.at[_tile_row(t)], xs_ref.at[_tile_row(dest_ref[j, t])],
                          sem).start(priority=j % 2)
            return carry

        def wait(t, carry):
            for j in range(TOP_K):
                _row_copy(src_ref.at[_tile_row(0)], xs_ref.at[_tile_row(0)], sem).wait()
            return carry

        lax.fori_loop(0, n, start, 0, unroll=4)
        lax.fori_loop(0, n, wait, 0, unroll=4)

    @pl.when(i < n_prompt_tiles)
    def _():
        scatter(up_ref)

    @pl.when(i == n_prompt_tiles)
    def _():
        scatter(us_ref)

    @pl.when(i == n_prompt_tiles + 1)
    def _():
        zero_scr[...] = jnp.zeros_like(zero_scr)

        def fill(e, start_not_wait):
            cnt = cnt_ref[e]
            pad = (-cnt) & (MOE_BLK - 1)
            off = ps_ref[e] + cnt

            def one(r, carry):
                cp = _row_copy(zero_scr.at[_tile_row(0)], xs_ref.at[_tile_row(off + r)], sem)
                if start_not_wait:
                    cp.start()
                else:
                    cp.wait()
                return carry

            lax.fori_loop(0, pad, one, 0)

        lax.fori_loop(0, N_EXPERTS, lambda e, c: (fill(e, True), c)[1], 0)
        lax.fori_loop(0, N_EXPERTS, lambda e, c: (fill(e, False), c)[1], 0)

        last = N_EXPERTS - 1
        used = (ps_ref[last] + cnt_ref[last] + (MOE_BLK - 1)) // MOE_BLK
        n_blocks = xs_ref.shape[0] // (MOE_BLK * SUBLANES)
        n_zero = zero_scr.shape[0] // SUBLANES

        def tail(start_not_wait):
            def one(b, carry):
                for r in range(0, MOE_BLK, n_zero):
                    cp = _row_copy(zero_scr, xs_ref.at[_tile_row(b * MOE_BLK + r, n_zero)], sem)
                    if start_not_wait:
                        cp.start()
                    else:
                        cp.wait()
                return carry

            lax.fori_loop(used, n_blocks, one, 0)

        tail(True)
        tail(False)


def _moe_dispatch(u_prompt, u_sample, dest, cnt, pstart):
    n_p = u_prompt.shape[0] // SUBLANES
    n_s = u_sample.shape[0] // SUBLANES
    n_tok = dest.shape[1]
    assert n_p % MOE_TILE == 0 and n_s <= MOE_TILE and n_tok == n_p + MOE_TILE
    npt = n_p // MOE_TILE
    cap = _moe_blocks(n_tok) * MOE_BLK
    grid_spec = pltpu.PrefetchScalarGridSpec(
        num_scalar_prefetch=2,
        grid=(npt + 2,),
        in_specs=[pl.BlockSpec((TOP_K, MOE_TILE), lambda i, c, p: (0, jnp.minimum(i, npt)),
                               memory_space=pltpu.SMEM),
                  pl.BlockSpec((MOE_TILE * SUBLANES, LANES), lambda i, c, p: (jnp.minimum(i, npt - 1), 0)),
                  pl.BlockSpec((n_s * SUBLANES, LANES), lambda i, c, p: (0, 0))],
        out_specs=pl.BlockSpec(memory_space=pl.ANY),
        scratch_shapes=[pltpu.VMEM((SUBLANES * SUBLANES, LANES), F32), pltpu.SemaphoreType.DMA(())],
    )
    return pl.pallas_call(
        functools.partial(_moe_dispatch_kernel, n_prompt_tiles=npt),
        grid_spec=grid_spec,
        out_shape=jax.ShapeDtypeStruct((cap * SUBLANES, LANES), F32),
        compiler_params=_params("arbitrary"),
        name="moe_dispatch",
    )(cnt, pstart, dest, u_prompt, u_sample)


def _moe_experts_kernel(be_ref, nu_ref, x_ref, wu_ref, bu_ref, wd_ref, bd_ref, y_ref, wu_bf, wd_bf):
    d_ff = wd_ref.shape[1]
    r = pl.program_id(0)
    active = r < nu_ref[0]
    new_expert = (r == 0) | (be_ref[r] != be_ref[jnp.maximum(r - 1, 0)])

    @pl.when(active & new_expert)
    def _():
        wu_bf[...] = wu_ref[0].astype(BF16)
        wd_bf[...] = wd_ref[0].astype(BF16)

    @pl.when(active)
    def _():
        x = _load_tile_rows(x_ref, MOE_BLK).astype(BF16)
        hdn = jnp.dot(x, wu_bf[...], preferred_element_type=F32) + bu_ref[0]
        glu = jnp.minimum(hdn[:, :d_ff], SWIGLU_LIMIT)
        lin = jnp.clip(hdn[:, d_ff:], -SWIGLU_LIMIT, SWIGLU_LIMIT)
        act = glu * _sigmoid(SWIGLU_ALPHA * glu) * (lin + 1.0)
        _store_tile_rows(y_ref, jnp.dot(act.astype(BF16), wd_bf[...], preferred_element_type=F32) + bd_ref[0])

    @pl.when(pl.program_id(0) >= nu_ref[0])
    def _():
        y_ref[...] = jnp.zeros_like(y_ref)


def _moe_experts(xs, blk_exp, n_used, w_up, b_up, w_down, b_down):
    blk = (MOE_BLK * SUBLANES, LANES)
    nb = xs.shape[0] // blk[0]
    last = lambda r, nu: jnp.minimum(r, nu[0] - 1)
    w_map = lambda r, be, nu: (be[last(r, nu)], 0, 0)
    x_map = lambda r, be, nu: (last(r, nu), 0)
    grid_spec = pltpu.PrefetchScalarGridSpec(
        num_scalar_prefetch=2,
        grid=(nb,),
        in_specs=[pl.BlockSpec(blk, x_map),
                  pl.BlockSpec((1,) + w_up.shape[1:], w_map), pl.BlockSpec((1,) + b_up.shape[1:], w_map),
                  pl.BlockSpec((1,) + w_down.shape[1:], w_map), pl.BlockSpec((1,) + b_down.shape[1:], w_map)],
        out_specs=pl.BlockSpec(blk, lambda r, be, nu: (r, 0)),
        scratch_shapes=[pltpu.VMEM(w_up.shape[1:], BF16), pltpu.VMEM(w_down.shape[1:], BF16)],
    )
    return pl.pallas_call(
        _moe_experts_kernel,
        grid_spec=grid_spec,
        out_shape=jax.ShapeDtypeStruct(xs.shape, F32),
        compiler_params=_params("arbitrary"),
        name="moe_experts",
    )(blk_exp, n_used, xs, w_up, b_up, w_down, b_down)


def _combine_kernel(dest_ref, dest_next_ref, h_ref, p_ref, gate_ref, ys_ref, wg_ref, wp_ref, gf_ref, o_ref,
                    y_scr, sem, *, final):
    tm = h_ref.shape[0]
    i = pl.program_id(0)
    slot = i % 2

    def gather(idx_ref, s):
        def start(t, carry):
            for j in range(TOP_K):
                _row_copy(ys_ref.at[_tile_row(idx_ref[j, t])], y_scr.at[s, j, _tile_row(t)],
                          sem.at[s]).start(priority=j % 2)
            return carry

        lax.fori_loop(0, tm, start, 0, unroll=4)

    @pl.when(i == 0)
    def _():
        gather(dest_ref, slot)

    @pl.when(i + 1 < pl.num_programs(0))
    def _():
        gather(dest_next_ref, 1 - slot)

    def wait(t, carry):
        for j in range(TOP_K):
            _row_copy(ys_ref.at[_tile_row(0)], y_scr.at[slot, 0, _tile_row(0)], sem.at[slot]).wait()
        return carry

    lax.fori_loop(0, tm, wait, 0, unroll=4)
    gate = gate_ref[...]
    h = h_ref[...]
    cols = []
    for c in range(SUBLANES):
        hc = h[:, c * LANES:(c + 1) * LANES]
        for j in range(TOP_K):
            hc = hc + gate[:, j:j + 1] * y_scr[slot, j, pl.ds(c, tm, stride=SUBLANES), :]
        cols.append(hc)
    h = jnp.concatenate(cols, axis=-1)
    emb_gate = _sigmoid(jnp.dot(h.astype(BF16), wg_ref[...], preferred_element_type=F32))
    emb = jnp.dot(p_ref[...].astype(BF16), wp_ref[...], preferred_element_type=F32)
    h = h + emb_gate * emb
    if final:
        ms = jnp.mean(h * h, axis=-1, keepdims=True)
        h = h * lax.rsqrt(ms + NORM_EPS) * gf_ref[...]
    o_ref[...] = h


def _combine(h1, p_emb, dest, gate_t, ys, w_gate, w_emb, g_final, final):
    m, d = h1.shape
    tm = _row_tile(m, 256)
    full = lambda a: pl.BlockSpec(a.shape, lambda i: (0,) * a.ndim)
    nt = m // tm
    return pl.pallas_call(
        functools.partial(_combine_kernel, final=final),
        grid=(nt,),
        in_specs=[pl.BlockSpec((TOP_K, tm), lambda i: (0, i), memory_space=pltpu.SMEM),
                  pl.BlockSpec((TOP_K, tm), lambda i: (0, jnp.minimum(i + 1, nt - 1)), memory_space=pltpu.SMEM),
                  pl.BlockSpec((tm, d), lambda i: (i, 0)),
                  pl.BlockSpec((tm, p_emb.shape[1]), lambda i: (i, 0)),
                  pl.BlockSpec((tm, TOP_K), lambda i: (i, 0)),
                  pl.BlockSpec(memory_space=pl.ANY),
                  full(w_gate), full(w_emb), full(g_final)],
        out_specs=pl.BlockSpec((tm, d), lambda i: (i, 0)),
        out_shape=jax.ShapeDtypeStruct((m, d), F32),
        scratch_shapes=[pltpu.VMEM((2, TOP_K, tm * SUBLANES, LANES), F32), pltpu.SemaphoreType.DMA((2,))],
        compiler_params=_params("arbitrary"),
        name="combine",
    )(dest, dest, h1, p_emb, gate_t, ys, w_gate, w_emb, g_final)


def kernel(x_prompt, x_sample, state_hgrn, cache_k, cache_v, cache_logf, page_table, p_prompt, p_sample, g_mix, w_in, hg_lb_logits, g_hg_out, fox_f_bias, g_fox_out, w_out, g_mlp, w_router, b_router, w_up, b_up, w_down, b_down, w_ple, w_ple_gate, g_final):
    depth = w_in.shape[0]
    bsz, seq, d = x_prompt.shape
    dec_b = x_sample.shape[0]
    n_p, n_s = bsz * seq, dec_b
    n_phys = cache_k.shape[1]
    lb_all = jnp.cumsum(jax.nn.softmax(hg_lb_logits.astype(F32), axis=0), axis=0)
    hp = x_prompt.reshape(n_p, d)
    hs = x_sample.reshape(n_s, d)
    n_main = 7 * HG_WIDTH
    outs = [[] for _ in range(8)]
    for i in range(depth):
        final = i == depth - 1
        w_main = w_in[i, :, :n_main].astype(BF16)
        w_ff = jnp.pad(w_in[i, :, n_main:], ((0, 0), (0, LANES - FOX_HEADS))).astype(BF16)
        f_bias = jnp.pad(fox_f_bias[i], (0, LANES - FOX_HEADS)).reshape(1, LANES)
        g_mix_i = g_mix[i].reshape(1, d)
        lb_i = lb_all[i].reshape(1, HG_WIDTH)
        g_hg = g_hg_out[i].reshape(1, HG_WIDTH)
        g_fox = g_fox_out[i].reshape(1, FOX_WIDTH)
        w_out_i = w_out[i].astype(BF16)
        g_mlp_i = g_mlp[i].reshape(1, d)
        w_r_t = w_router[i].T
        b_r = b_router[i].reshape(N_EXPERTS, 1)

        q, f, v, gt, fq, fk, fv, lf, kb, vb, c = _inproj(hp, g_mix_i, w_main, w_ff, lb_i, f_bias, seq)
        oh, sp = _hgrn_prompt(q, f, v, gt, g_hg, bsz, seq)
        of = _fox_prompt(fq, kb, vb, c, c.T, g_fox, bsz, seq)
        h1_p, u_p, idx_p, gate_p = _outproj(oh, of, hp, w_out_i, g_mlp_i, w_r_t, b_r)
        outs[0].append(sp)
        outs[1].append(fk.reshape(bsz, seq, FOX_HEADS, FOX_DH))
        outs[2].append(fv.reshape(bsz, seq, FOX_HEADS, FOX_DH))
        outs[3].append(lf.reshape(bsz, seq, FOX_HEADS))

        q, f, v, gt, fq, fk, fv, lf = _inproj(hs, g_mix_i, w_main, w_ff, lb_i, f_bias, 1)
        oh, ss = _hgrn_step(q, f, v, gt, g_hg, state_hgrn[i])
        kv_t = lambda a: jnp.transpose(a[i], (0, 2, 3, 1)).reshape(n_phys, FOX_WIDTH, PAGE_SIZE)
        of = _fox_decode(fq, fk, fv, lf, g_fox, kv_t(cache_k), kv_t(cache_v),
                         jnp.transpose(cache_logf[i], (0, 2, 1)), page_table)
        h1_s, u_s, idx_s, gate_s = _outproj(oh, of, hs, w_out_i, g_mlp_i, w_r_t, b_r)
        outs[4].append(ss)
        outs[5].append(fk.reshape(dec_b, 1, FOX_HEADS, FOX_DH))
        outs[6].append(fv.reshape(dec_b, 1, FOX_HEADS, FOX_DH))
        outs[7].append(lf.reshape(dec_b, 1, FOX_HEADS))

        n_tile = -(-(n_p + n_s) // MOE_TILE) * MOE_TILE
        idx_all = jnp.concatenate(
            [idx_p, idx_s, jnp.full((TOP_K, n_tile - n_p - n_s), N_EXPERTS, jnp.int32)], axis=1)
        dest, blk_exp, n_used, cnt, pstart = _moe_plan(idx_all)
        xs = _moe_dispatch(u_p, u_s, dest, cnt, pstart)
        ys = _moe_experts(xs, blk_exp, n_used, w_up[i], b_up[i].reshape(N_EXPERTS, 1, -1),
                          w_down[i], b_down[i].reshape(N_EXPERTS, 1, -1))
        w_pg = w_ple_gate[i].astype(BF16)
        w_p = w_ple[i].astype(BF16)
        g_fin = g_final.reshape(1, d)
        hp = _combine(h1_p, p_prompt[i].reshape(n_p, -1), dest[:, :n_p], gate_p.T, ys, w_pg, w_p, g_fin, final)
        hs = _combine(h1_s, p_sample[i].reshape(n_s, -1), dest[:, n_p:n_p + n_s], gate_s.T, ys, w_pg, w_p, g_fin,
                      final)
    st = [jnp.stack(o) for o in outs]
    return (hp.reshape(bsz, seq, d), hs.reshape(dec_b, 1, d), *st)
```

```python
import functools

import jax
import jax.numpy as jnp
from jax import lax
from jax.experimental import pallas as pl
from jax.experimental.pallas import tpu as pltpu

F32 = jnp.float32
BF16 = jnp.bfloat16

HG_HEADS = 4
HG_DK = 128
HG_DV = 128
HG_WIDTH = HG_HEADS * HG_DK
FOX_HEADS = 8
FOX_DH = 64
FOX_WIDTH = FOX_HEADS * FOX_DH
N_EXPERTS = 32
TOP_K = 4
PAGE_SIZE = 128
NORM_EPS = 1e-6
SWIGLU_ALPHA = 1.702
SWIGLU_LIMIT = 7.0

LANES = 128
SUBLANES = 8
NEG_BIG = -1e30
VMEM_LIMIT = 56 * 1024 * 1024

HG_CHUNK = 16
HG_BLOCK = 128
FOX_TQ = 512
FOX_TK = 512

def _params(*sem):
    return pltpu.CompilerParams(dimension_semantics=sem, vmem_limit_bytes=VMEM_LIMIT)


def _row_tile(m, pref):
    return pref if m % pref == 0 else m


def _split3(x):
    hi = x.astype(BF16)
    r1 = x - hi.astype(F32)
    mid = r1.astype(BF16)
    lo = (r1 - mid.astype(F32)).astype(BF16)
    return hi, mid, lo


def _dot01(mask_bf16, x):
    hi, mid, lo = _split3(x)
    d = lambda a: jnp.dot(mask_bf16, a, preferred_element_type=F32)
    return d(hi) + d(mid) + d(lo)


def _tri_lower(n, strict=False):
    r = lax.broadcasted_iota(jnp.int32, (n, n), 0)
    c = lax.broadcasted_iota(jnp.int32, (n, n), 1)
    m = (c < r) if strict else (c <= r)
    return jnp.where(m, 1.0, 0.0).astype(BF16)


def _sigmoid(x):
    return 1.0 / (1.0 + jnp.exp(-x))


def _silu(x):
    return x * _sigmoid(x)


def _log_sigmoid(x):
    return jnp.minimum(x, 0.0) - jnp.log1p(jnp.exp(-jnp.abs(x)))


def _inproj_kernel(x_ref, g_ref, w_ref, wf_ref, lb_ref, fb_ref,
                   q_ref, f_ref, i_ref, gt_ref, fq_ref, lf_ref, *rest, tiles_per_seq):
    x = x_ref[...]
    ms = jnp.mean(x * x, axis=-1, keepdims=True)
    u = (x * lax.rsqrt(ms + NORM_EPS) * g_ref[...]).astype(BF16)

    def proj(j):
        return jnp.dot(u, w_ref[:, j * HG_WIDTH:(j + 1) * HG_WIDTH], preferred_element_type=F32)

    q_ref[...] = _silu(proj(0))
    lb = lb_ref[...]
    f_ref[...] = lb + (1.0 - lb) * _sigmoid(proj(1))
    i_ref[...] = proj(2)
    gt_ref[...] = _silu(proj(3))
    fq_ref[...] = (proj(4) * (FOX_DH ** -0.5)).astype(BF16)
    fk = proj(5)
    fv = proj(6)
    ff = jnp.dot(u, wf_ref[...], preferred_element_type=F32) + fb_ref[...]
    lf = _log_sigmoid(ff)
    lf_ref[...] = lf[:, :FOX_HEADS]
    if not tiles_per_seq:
        fk_ref, fv_ref = rest
        fk_ref[...] = fk
        fv_ref[...] = fv
    else:
        kh_ref, vh_ref, kb_ref, vb_ref, c_ref, carry_ref = rest
        tm = x.shape[0]
        for h in range(FOX_HEADS):
            cols = slice(h * FOX_DH, (h + 1) * FOX_DH)
            kh_ref[pl.ds(h, tm, stride=FOX_HEADS), :] = fk[:, cols]
            vh_ref[pl.ds(h, tm, stride=FOX_HEADS), :] = fv[:, cols]
        kb_ref[...] = fk.astype(BF16)
        vb_ref[...] = fv.astype(BF16)

        @pl.when(pl.program_id(0) % tiles_per_seq == 0)
        def _():
            carry_ref[...] = jnp.zeros_like(carry_ref)

        c = _dot01(_tri_lower(tm), lf) + carry_ref[0:1, :]
        carry_ref[...] = jnp.broadcast_to(c[tm - 1:tm, :], carry_ref.shape)
        c_ref[...] = c[:, :FOX_HEADS]


def _inproj(x2d, g_mix, w_main, w_ff, lb, f_bias, seq_len):
    m, d = x2d.shape
    tm = _row_tile(m if seq_len == 1 else seq_len, 512)
    tiles_per_seq = 0 if seq_len == 1 else seq_len // tm
    row = lambda w: pl.BlockSpec((tm, w), lambda i: (i, 0))
    full = lambda a: pl.BlockSpec(a.shape, lambda i: (0,) * a.ndim)
    wide = jax.ShapeDtypeStruct((m, HG_WIDTH), F32)
    wide_bf = jax.ShapeDtypeStruct((m, HG_WIDTH), BF16)
    narrow = jax.ShapeDtypeStruct((m, FOX_HEADS), F32)
    out_specs = [row(HG_WIDTH)] * 5 + [row(FOX_HEADS)]
    out_shape = [wide, wide, wide, wide, wide_bf, narrow]
    scratch = []
    if not tiles_per_seq:
        out_specs += [row(HG_WIDTH), row(HG_WIDTH)]
        out_shape += [wide, wide]
    else:
        assert FOX_HEADS == SUBLANES
        heads = jax.ShapeDtypeStruct((m * FOX_HEADS, FOX_DH), F32)
        head_rows = pl.BlockSpec((tm * FOX_HEADS, FOX_DH), lambda i: (i, 0))
        out_specs += [head_rows, head_rows, row(HG_WIDTH), row(HG_WIDTH), row(FOX_HEADS)]
        out_shape += [heads, heads, wide_bf, wide_bf, narrow]
        scratch = [pltpu.VMEM((SUBLANES, LANES), F32)]
    return pl.pallas_call(
        functools.partial(_inproj_kernel, tiles_per_seq=tiles_per_seq),
        grid=(m // tm,),
        in_specs=[row(d), full(g_mix), full(w_main), full(w_ff), full(lb), full(f_bias)],
        out_specs=out_specs,
        out_shape=out_shape,
        scratch_shapes=scratch,
        compiler_params=_params("arbitrary"),
        name="inproj",
    )(x2d, g_mix, w_main, w_ff, lb, f_bias)


def _hgrn_kernel(q_ref, f_ref, v_ref, gt_ref, gn_ref, o_ref, s_ref, s_scr, b_scr):
    tb = q_ref.shape[0]

    @pl.when(pl.program_id(1) == 0)
    def _():
        s_scr[...] = jnp.zeros_like(s_scr)

    tri = _tri_lower(HG_BLOCK)
    row_id = lax.broadcasted_iota(jnp.int32, (HG_CHUNK, 1), 0)
    ones_kk = jnp.ones((HG_DK, HG_DK), BF16)

    def block_body(blk, carry):
        r_blk = pl.multiple_of(blk * HG_BLOCK, HG_BLOCK)
        b_scr[...] = _dot01(tri, jnp.log(f_ref[pl.ds(r_blk, HG_BLOCK), :]))

        def chunk_body(ci, carry2):
            r_loc = pl.multiple_of(ci * HG_CHUNK, HG_CHUNK)
            r0 = r_blk + r_loc
            for h in range(HG_HEADS):
                cols = slice(h * HG_DK, (h + 1) * HG_DK)
                q = q_ref[pl.ds(r0, HG_CHUNK), cols]
                k = 1.0 - f_ref[pl.ds(r0, HG_CHUNK), cols]
                v = v_ref[pl.ds(r0, HG_CHUNK), cols]
                bb = b_scr[pl.ds(r_loc, HG_CHUNK), cols]
                b_prev = bb[0:1, :] - jnp.log(f_ref[pl.ds(r0, 1), cols])
                b = bb - b_prev
                st_old = s_scr[h]
                o = lax.dot_general((q * jnp.exp(b)).astype(BF16), st_old.astype(BF16), (((1,), (1,)), ((), ())),
                                    preferred_element_type=F32)
                terms, first_row = [], []
                for s in range(HG_CHUNK):
                    t0 = (s // SUBLANES) * SUBLANES
                    d = jnp.where(row_id[t0:] >= s, b[t0:] - b[s:s + 1, :], NEG_BIG)
                    terms.append(jnp.exp(d) * q[t0:] * k[s:s + 1, :])
                    first_row.append(t0)
                a_all = jnp.dot(jnp.concatenate(terms, axis=0).astype(BF16), ones_kk, preferred_element_type=F32)
                o_rows = [o[t:t + SUBLANES] for t in range(0, HG_CHUNK, SUBLANES)]
                at = 0
                for s in range(HG_CHUNK):
                    for g in range(first_row[s] // SUBLANES, HG_CHUNK // SUBLANES):
                        o_rows[g] = o_rows[g] + a_all[at:at + SUBLANES] * v[s:s + 1, :]
                        at += SUBLANES
                o = jnp.concatenate(o_rows, axis=0)
                b_last = b[HG_CHUNK - 1:HG_CHUNK, :]
                kd = (k * jnp.exp(b_last - b)).astype(BF16)
                upd_t = lax.dot_general(v.astype(BF16), kd, (((0,), (0,)), ((), ())),
                                        preferred_element_type=F32)
                s_scr[h] = st_old * jnp.exp(b_last) + upd_t
                ms = jnp.mean(o * o, axis=-1, keepdims=True)
                o = o * lax.rsqrt(ms + NORM_EPS) * gn_ref[:, cols] * gt_ref[pl.ds(r0, HG_CHUNK), cols]
                o_ref[pl.ds(r0, HG_CHUNK), cols] = o.astype(BF16)
            return carry2

        return lax.fori_loop(0, HG_BLOCK // HG_CHUNK, chunk_body, carry, unroll=2)

    lax.fori_loop(0, tb // HG_BLOCK, block_body, 0)

    @pl.when(pl.program_id(1) == pl.num_programs(1) - 1)
    def _():
        for h in range(HG_HEADS):
            s_ref[0, h] = jnp.transpose(s_scr[h])


def _hgrn_prompt(q, f, v, gt, g_norm, batch, seq):
    tb = _row_tile(seq, 512)
    assert tb % HG_BLOCK == 0
    nt = seq // tb
    row = pl.BlockSpec((tb, HG_WIDTH), lambda b, t: (b * nt + t, 0))
    return pl.pallas_call(
        _hgrn_kernel,
        grid=(batch, nt),
        in_specs=[row, row, row, row, pl.BlockSpec((1, HG_WIDTH), lambda b, t: (0, 0))],
        out_specs=[row, pl.BlockSpec((1, HG_HEADS, HG_DK, HG_DV), lambda b, t: (b, 0, 0, 0))],
        out_shape=[jax.ShapeDtypeStruct((batch * seq, HG_WIDTH), BF16),
                   jax.ShapeDtypeStruct((batch, HG_HEADS, HG_DK, HG_DV), F32)],
        scratch_shapes=[pltpu.VMEM((HG_HEADS, HG_DK, HG_DV), F32), pltpu.VMEM((HG_BLOCK, HG_WIDTH), F32)],
        compiler_params=_params("arbitrary", "arbitrary"),
        name="hgrn_prompt",
    )(q, f, v, gt, g_norm)


def _fox_prompt_kernel(q_ref, k_ref, v_ref, cq_ref, ck_ref, gn_ref, o_ref, m_scr, l_scr, acc_scr):
    pair = pl.program_id(1)
    qi = pl.program_id(2)
    ki = pl.program_id(3)
    tq = q_ref.shape[0]
    tk = k_ref.shape[0]

    @pl.when(ki == 0)
    def _():
        m_scr[...] = jnp.full_like(m_scr, NEG_BIG)
        l_scr[...] = jnp.zeros_like(l_scr)
        acc_scr[...] = jnp.zeros_like(acc_scr)

    last_k = ((qi + 1) * tq - 1) // tk

    @pl.when(ki <= last_k)
    def _():
        lane = lax.broadcasted_iota(jnp.int32, (1, LANES), 1)
        first = lane < FOX_DH
        q = q_ref[...]
        k = k_ref[...]
        v = v_ref[...]
        q_pos = qi * tq + lax.broadcasted_iota(jnp.int32, (tq, tk), 0)
        k_pos = ki * tk + lax.broadcasted_iota(jnp.int32, (tq, tk), 1)
        causal = k_pos <= q_pos
        cq_all = cq_ref[...]
        ck_all = ck_ref[...]
        head_col = lax.broadcasted_iota(jnp.int32, (1, FOX_HEADS), 1)
        head_row = lax.broadcasted_iota(jnp.int32, (FOX_HEADS, 1), 0)
        pv = []
        for j in range(2):
            h = 2 * pair + j
            qh = jnp.where(first if j == 0 else ~first, q, jnp.zeros_like(q))
            s = lax.dot_general(qh, k, (((1,), (1,)), ((), ())), preferred_element_type=F32)
            cq = jnp.sum(jnp.where(head_col == h, cq_all, 0.0), axis=1, keepdims=True)
            ck = jnp.sum(jnp.where(head_row == h, ck_all, 0.0), axis=0, keepdims=True)
            s = jnp.where(causal, s + cq - ck, NEG_BIG)
            m_old = m_scr[j]
            m_new = jnp.maximum(m_old, jnp.max(s, axis=-1, keepdims=True))
            alpha = jnp.exp(m_old - m_new)
            p = jnp.exp(s - m_new)
            l_scr[j] = alpha * l_scr[j] + jnp.sum(p, axis=-1, keepdims=True)
            m_scr[j] = m_new
            pv.append((alpha, jnp.dot(p.astype(BF16), v, preferred_element_type=F32)))
        alpha = jnp.where(first, pv[0][0], pv[1][0])
        acc_scr[...] = alpha * acc_scr[...] + jnp.where(first, pv[0][1], pv[1][1])

    @pl.when(ki == last_k)
    def _():
        lane = lax.broadcasted_iota(jnp.int32, (1, LANES), 1)
        first = lane < FOX_DH
        o = acc_scr[...] / jnp.where(first, l_scr[0], l_scr[1])
        sq = o * o
        ss0 = jnp.sum(jnp.where(first, sq, 0.0), axis=-1, keepdims=True)
        ss1 = jnp.sum(jnp.where(first, 0.0, sq), axis=-1, keepdims=True)
        ms = jnp.where(first, ss0, ss1) * (1.0 / FOX_DH)
        o_ref[...] = (o * lax.rsqrt(ms + NORM_EPS) * gn_ref[...]).astype(BF16)


def _fox_prompt(q_bf, k_bf, v_bf, c_col, c_row, g_norm, batch, seq):
    tq = _row_tile(seq, FOX_TQ)
    tk = _row_tile(seq, FOX_TK)
    nq, nk = seq // tq, seq // tk
    pairs = FOX_HEADS // 2
    k_blk = lambda i, j: jnp.minimum(j, ((i + 1) * tq - 1) // tk)
    kv_map = lambda b, p, i, j: (b * nk + k_blk(i, j), p)
    return pl.pallas_call(
        _fox_prompt_kernel,
        grid=(batch, pairs, nq, nk),
        in_specs=[pl.BlockSpec((tq, LANES), lambda b, p, i, j: (b * nq + i, p)),
                  pl.BlockSpec((tk, LANES), kv_map),
                  pl.BlockSpec((tk, LANES), kv_map),
                  pl.BlockSpec((tq, FOX_HEADS), lambda b, p, i, j: (b * nq + i, 0)),
                  pl.BlockSpec((FOX_HEADS, tk), lambda b, p, i, j: (0, b * nk + k_blk(i, j))),
                  pl.BlockSpec((1, LANES), lambda b, p, i, j: (0, p))],
        out_specs=pl.BlockSpec((tq, LANES), lambda b, p, i, j: (b * nq + i, p)),
        out_shape=jax.ShapeDtypeStruct((batch * seq, FOX_WIDTH), BF16),
        scratch_shapes=[pltpu.VMEM((2, tq, 1), F32), pltpu.VMEM((2, tq, 1), F32),
                        pltpu.VMEM((tq, LANES), F32)],
        compiler_params=_params("arbitrary", "arbitrary", "arbitrary", "arbitrary"),
        name="fox_prompt",
    )(q_bf, k_bf, v_bf, c_col, c_row, g_norm)


def _hgrn_step_kernel(q_ref, f_ref, v_ref, gt_ref, gn_ref, s_ref, o_ref, so_ref):
    rows = q_ref.shape[0]

    def col(row):
        return jnp.transpose(jnp.broadcast_to(row, (HG_DK, HG_DK)))

    for r in range(rows):
        for h in range(HG_HEADS):
            cols = slice(h * HG_DK, (h + 1) * HG_DK)
            f_col = col(f_ref[r:r + 1, cols])
            q_col = col(q_ref[r:r + 1, cols])
            s_new = f_col * s_ref[r, h] + (1.0 - f_col) * v_ref[r:r + 1, cols]
            so_ref[r, h] = s_new
            o = jnp.sum(s_new * q_col, axis=0, keepdims=True)
            ms = jnp.mean(o * o, axis=-1, keepdims=True)
            o = o * lax.rsqrt(ms + NORM_EPS) * gn_ref[:, cols] * gt_ref[r:r + 1, cols]
            o_ref[r:r + 1, cols] = o.astype(BF16)


def _hgrn_step(q, f, v, gt, g_norm, state):
    n = q.shape[0]
    rows = SUBLANES
    assert n % rows == 0
    row = pl.BlockSpec((rows, HG_WIDTH), lambda i: (i, 0))
    st = pl.BlockSpec((rows, HG_HEADS, HG_DK, HG_DV), lambda i: (i, 0, 0, 0))
    return pl.pallas_call(
        _hgrn_step_kernel,
        grid=(n // rows,),
        in_specs=[row, row, row, row, pl.BlockSpec((1, HG_WIDTH), lambda i: (0, 0)), st],
        out_specs=[row, st],
        out_shape=[jax.ShapeDtypeStruct((n, HG_WIDTH), BF16), jax.ShapeDtypeStruct(state.shape, F32)],
        compiler_params=_params("arbitrary"),
        name="hgrn_step",
    )(q, f, v, gt, g_norm, state)


FOX_PAGES_PER_STEP = 16


def _dot01_rhs(x, mask_bf16):
    hi, mid, lo = _split3(x)
    d = lambda a: jnp.dot(a, mask_bf16, preferred_element_type=F32)
    return d(hi) + d(mid) + d(lo)


def _fox_decode_kernel(pt_ref, q_ref, kn_ref, vn_ref, lfn_ref, gn_ref, *refs):
    del pt_ref
    npg = FOX_PAGES_PER_STEP
    k_refs, v_refs, lf_refs = refs[:npg], refs[npg:2 * npg], refs[2 * npg:3 * npg]
    o_ref, m_scr, l_scr, r_scr, acc_scr = refs[3 * npg:]
    c = pl.program_id(1)
    head = lax.broadcasted_iota(jnp.int32, (FOX_HEADS, FOX_WIDTH), 0)
    lane = lax.broadcasted_iota(jnp.int32, (FOX_HEADS, FOX_WIDTH), 1)
    own = lane // FOX_DH == head
    q_rows = jnp.where(own, q_ref[0].astype(F32), 0.0)
    lf_new = lfn_ref[0]

    @pl.when(c == 0)
    def _():
        k_new = kn_ref[0].astype(BF16).astype(F32)
        m_scr[...] = jnp.sum(q_rows * k_new, axis=1, keepdims=True)
        l_scr[...] = jnp.ones_like(l_scr)
        r_scr[...] = jnp.zeros_like(r_scr)
        acc_scr[...] = jnp.broadcast_to(vn_ref[0], acc_scr.shape)

    q_bf = q_rows.astype(BF16)
    rr = lax.broadcasted_iota(jnp.int32, (PAGE_SIZE, PAGE_SIZE), 0)
    cc = lax.broadcasted_iota(jnp.int32, (PAGE_SIZE, PAGE_SIZE), 1)
    later = jnp.where(rr > cc, 1.0, 0.0).astype(BF16)

    def stack(x):
        return jnp.concatenate([x[:, g * PAGE_SIZE:(g + 1) * PAGE_SIZE] for g in range(npg)], axis=0)

    def unstack(x):
        return jnp.concatenate([x[g * FOX_HEADS:(g + 1) * FOX_HEADS] for g in range(npg)], axis=1)

    def per_head(x, op):
        out = x[0:FOX_HEADS]
        for g in range(1, npg):
            out = op(out, x[g * FOX_HEADS:(g + 1) * FOX_HEADS])
        return out

    k_cat = jnp.concatenate([k_refs[g][0].astype(BF16) for g in range(npg)], axis=1)
    v_cat = jnp.concatenate([v_refs[g][0].astype(BF16) for g in range(npg)], axis=1)
    lf = jnp.concatenate([lf_refs[g][0] for g in range(npg)], axis=0)
    page_mass = jnp.sum(lf, axis=1, keepdims=True)
    run = r_scr[...]
    before = []
    for g in range(npg):
        before.append(run)
        run = run + page_mass[g * FOX_HEADS:(g + 1) * FOX_HEADS]
    r_scr[...] = run
    bias = _dot01_rhs(lf, later) + jnp.concatenate(before, axis=0) + jnp.concatenate([lf_new] * npg, axis=0)
    s = stack(jnp.dot(q_bf, k_cat, preferred_element_type=F32)) + bias

    m_old = m_scr[...]
    m_new = jnp.maximum(m_old, per_head(jnp.max(s, axis=1, keepdims=True), jnp.maximum))
    alpha = jnp.exp(m_old - m_new)
    p = jnp.exp(s - jnp.concatenate([m_new] * npg, axis=0))
    l_scr[...] = alpha * l_scr[...] + per_head(jnp.sum(p, axis=1, keepdims=True), jnp.add)
    m_scr[...] = m_new
    pv = lax.dot_general(unstack(p).astype(BF16), v_cat, (((1,), (1,)), ((), ())),
                         preferred_element_type=F32)
    acc_scr[...] = alpha * acc_scr[...] + pv

    @pl.when(c == pl.num_programs(1) - 1)
    def _():
        o = jnp.where(own, acc_scr[...] / l_scr[...], 0.0)
        ms = jnp.sum(o * o, axis=1, keepdims=True) * (1.0 / FOX_DH)
        o = o * lax.rsqrt(ms + NORM_EPS)
        o_ref[0] = (jnp.sum(o, axis=0, keepdims=True) * gn_ref[...]).astype(BF16)


def _fox_decode(q_bf, k_new, v_new, lf_new, g_norm, cache_kt, cache_vt, cache_lft, page_table):
    n, n_pages = page_table.shape
    npg = FOX_PAGES_PER_STEP
    assert n_pages % npg == 0
    n_chunks = n_pages // npg

    def page_map(g):
        return lambda b, c, pt: (pt[b, (n_chunks - 1 - c) * npg + (npg - 1 - g)], 0, 0)

    row3 = lambda h, w: pl.BlockSpec((1, h, w), lambda b, c, pt: (b, 0, 0))
    kv_specs = [pl.BlockSpec((1, FOX_WIDTH, PAGE_SIZE), page_map(g)) for g in range(npg)]
    lf_specs = [pl.BlockSpec((1, FOX_HEADS, PAGE_SIZE), page_map(g)) for g in range(npg)]
    stat = pltpu.VMEM((FOX_HEADS, 1), F32)
    grid_spec = pltpu.PrefetchScalarGridSpec(
        num_scalar_prefetch=1,
        grid=(n, n_chunks),
        in_specs=[row3(1, FOX_WIDTH), row3(1, FOX_WIDTH), row3(1, FOX_WIDTH), row3(FOX_HEADS, 1),
                  pl.BlockSpec((1, FOX_WIDTH), lambda b, c, pt: (0, 0))] + kv_specs + kv_specs + lf_specs,
        out_specs=row3(1, FOX_WIDTH),
        scratch_shapes=[stat, stat, stat, pltpu.VMEM((FOX_HEADS, FOX_WIDTH), F32)],
    )
    out = pl.pallas_call(
        _fox_decode_kernel,
        grid_spec=grid_spec,
        out_shape=jax.ShapeDtypeStruct((n, 1, FOX_WIDTH), BF16),
        compiler_params=_params("arbitrary", "arbitrary"),
        name="fox_decode",
    )(page_table, q_bf.reshape(n, 1, FOX_WIDTH), k_new.reshape(n, 1, FOX_WIDTH), v_new.reshape(n, 1, FOX_WIDTH),
      lf_new.reshape(n, FOX_HEADS, 1), g_norm, *([cache_kt] * npg), *([cache_vt] * npg), *([cache_lft] * npg))
    return out.reshape(n, FOX_WIDTH)


def _outproj_kernel(oh_ref, of_ref, x_ref, w_ref, g_ref, wr_ref, br_ref, h_ref, u_ref, idx_ref, gate_ref):
    mix_in = jnp.concatenate([oh_ref[...], of_ref[...]], axis=-1)
    h = x_ref[...] + jnp.dot(mix_in, w_ref[...], preferred_element_type=F32)
    h_ref[...] = h
    ms = jnp.mean(h * h, axis=-1, keepdims=True)
    u = h * lax.rsqrt(ms + NORM_EPS) * g_ref[...]
    _store_tile_rows(u_ref, u)
    vals = lax.dot_general(wr_ref[...], u, (((1,), (1,)), ((), ())), preferred_element_type=F32,
                           precision=lax.Precision.HIGHEST) + br_ref[...]
    expert = lax.broadcasted_iota(jnp.int32, vals.shape, 0)
    top_v, top_i = [], []
    for _ in range(TOP_K):
        m = jnp.max(vals, axis=0, keepdims=True)
        sel = jnp.min(jnp.where(vals == m, expert, N_EXPERTS), axis=0, keepdims=True)
        vals = jnp.where(expert == sel, -jnp.inf, vals)
        top_v.append(m)
        top_i.append(sel)
    ex = [jnp.exp(t - top_v[0]) for t in top_v]
    den = ex[0] + ex[1] + ex[2] + ex[3]
    idx_ref[...] = jnp.concatenate(top_i, axis=0)
    gate_ref[...] = jnp.concatenate([x / den for x in ex], axis=0)


def _outproj(oh, of, x, w_out, g_mlp, w_router_t, b_router):
    m, d = x.shape
    assert d == SUBLANES * LANES
    tm = _row_tile(m, 512)
    row = lambda w: pl.BlockSpec((tm, w), lambda i: (i, 0))
    full = lambda a: pl.BlockSpec(a.shape, lambda i: (0,) * a.ndim)
    colblk = pl.BlockSpec((TOP_K, tm), lambda i: (0, i))
    return pl.pallas_call(
        _outproj_kernel,
        grid=(m // tm,),
        in_specs=[row(HG_WIDTH), row(FOX_WIDTH), row(d), full(w_out), full(g_mlp), full(w_router_t),
                  full(b_router)],
        out_specs=[row(d), pl.BlockSpec((tm * SUBLANES, LANES), lambda i: (i, 0)), colblk, colblk],
        out_shape=[jax.ShapeDtypeStruct((m, d), F32), jax.ShapeDtypeStruct((m * SUBLANES, LANES), F32),
                   jax.ShapeDtypeStruct((TOP_K, m), jnp.int32), jax.ShapeDtypeStruct((TOP_K, m), F32)],
        compiler_params=_params("arbitrary"),
        name="outproj",
    )(oh, of, x, w_out, g_mlp, w_router_t, b_router)


MOE_TILE = 512
MOE_BLK = 256


def _moe_plan_kernel(idx_ref, dest_ref, be_ref, nu_ref, cnt_ref, ps_ref, cnt_scr, run_scr, ps_scr):
    phase = pl.program_id(0)
    step = pl.program_id(1)
    tm = idx_ref.shape[1]
    expert = lax.broadcasted_iota(jnp.int32, (N_EXPERTS, tm), 0)
    idx = idx_ref[...]
    onehot = [expert == idx[j:j + 1, :] for j in range(TOP_K)]

    @pl.when((phase == 0) & (step == 0))
    def _():
        cnt_scr[...] = jnp.zeros_like(cnt_scr)

    @pl.when(phase == 0)
    def _():
        tot = jnp.zeros((N_EXPERTS, 1), F32)
        for oh in onehot:
            tot = tot + jnp.sum(jnp.where(oh, 1.0, 0.0), axis=1, keepdims=True)
        cnt_scr[...] = cnt_scr[...] + tot

    @pl.when((phase == 1) & (step == 0))
    def _():
        cnt = cnt_scr[...]
        padded = jnp.floor((cnt + (MOE_BLK - 1)) * (1.0 / MOE_BLK)) * MOE_BLK
        pstart = _dot01(_tri_lower(N_EXPERTS, strict=True), padded)
        pend = pstart + padded
        ps_scr[...] = pstart
        run_scr[...] = jnp.zeros_like(run_scr)
        cnt_ref[...] = cnt.astype(jnp.int32)
        ps_ref[...] = pstart.astype(jnp.int32)
        nb = be_ref.shape[1]
        row0 = lax.broadcasted_iota(jnp.int32, (1, nb), 1).astype(F32) * MOE_BLK
        be = jnp.sum(jnp.where(pend[:, 0:1] <= row0, 1.0, 0.0), axis=0, keepdims=True)
        be_ref[...] = jnp.minimum(be, N_EXPERTS - 1).astype(jnp.int32)
        total = jnp.max(pend[:, 0:1], axis=0, keepdims=True)
        nu_ref[...] = jnp.broadcast_to(total * (1.0 / MOE_BLK), nu_ref.shape).astype(jnp.int32)

    @pl.when(phase == 1)
    def _():
        r = lax.broadcasted_iota(jnp.int32, (tm, tm), 0)
        c = lax.broadcasted_iota(jnp.int32, (tm, tm), 1)
        before = jnp.where(r < c, 1.0, 0.0).astype(BF16)
        run = run_scr[:, 0:1]
        base = ps_scr[:, 0:1]
        rows = []
        for oh in onehot:
            ohf = jnp.where(oh, 1.0, 0.0)
            prefix = jnp.dot(ohf.astype(BF16), before, preferred_element_type=F32)
            rows.append(jnp.sum(ohf * (base + run + prefix), axis=0, keepdims=True))
            run = run + jnp.sum(ohf, axis=1, keepdims=True)
        run_scr[...] = jnp.broadcast_to(run, run_scr.shape)
        dest_ref[...] = jnp.concatenate(rows, axis=0).astype(jnp.int32)


def _moe_blocks(n_tok):
    return n_tok * TOP_K // MOE_BLK + N_EXPERTS


def _moe_plan(idx_all):
    n_tok = idx_all.shape[1]
    nb = -(-_moe_blocks(n_tok) // LANES) * LANES
    lane_i32 = jax.ShapeDtypeStruct((N_EXPERTS, LANES), jnp.int32)
    const = lambda shape: pl.BlockSpec(shape, lambda p, s: (0, 0))
    dest, be, nu, cnt, ps = pl.pallas_call(
        _moe_plan_kernel,
        grid=(2, n_tok // MOE_TILE),
        in_specs=[pl.BlockSpec((TOP_K, MOE_TILE), lambda p, s: (0, s))],
        out_specs=[pl.BlockSpec((TOP_K, MOE_TILE), lambda p, s: (0, s * p)),
                   const((1, nb)), const((1, LANES)), const((N_EXPERTS, LANES)), const((N_EXPERTS, LANES))],
        out_shape=[jax.ShapeDtypeStruct((TOP_K, n_tok), jnp.int32), jax.ShapeDtypeStruct((1, nb), jnp.int32),
                   jax.ShapeDtypeStruct((1, LANES), jnp.int32), lane_i32, lane_i32],
        scratch_shapes=[pltpu.VMEM((N_EXPERTS, LANES), F32)] * 3,
        compiler_params=_params("arbitrary", "arbitrary"),
        name="moe_plan",
    )(idx_all)
    return dest, be[0], nu[0, :1], cnt[:, 0], ps[:, 0]


def _row_copy(src, dst, sem):
    return pltpu.make_async_copy(src, dst, sem)


def _tile_row(t, n=1):
    return pl.ds(pl.multiple_of(t * SUBLANES, SUBLANES), n * SUBLANES)


def _load_tile_rows(ref, n):
    return jnp.concatenate([ref[pl.ds(c, n, stride=SUBLANES), :] for c in range(SUBLANES)], axis=-1)


def _store_tile_rows(ref, val):
    n = val.shape[0]
    for c in range(SUBLANES):
        ref[pl.ds(c, n, stride=SUBLANES), :] = val[:, c * LANES:(c + 1) * LANES]


def _moe_dispatch_kernel(cnt_ref, ps_ref, dest_ref, up_ref, us_ref, xs_ref, zero_scr, sem, *, n_prompt_tiles):
    i = pl.program_id(0)

    def scatter(src_ref):
        n = src_ref.shape[0] // SUBLANES

        def start(t, carry):
            for j in range(TOP_K):
                _row_copy(src_ref.at[_tile_row(t)], xs_ref.at[_tile_row(dest_ref[j, t])],
                          sem).start(priority=j % 2)
            return carry

        def wait(t, carry):
            for j in range(TOP_K):
                _row_copy(src_ref.at[_tile_row(0)], xs_ref.at[_tile_row(0)], sem).wait()
            return carry

        lax.fori_loop(0, n, start, 0, unroll=4)
        lax.fori_loop(0, n, wait, 0, unroll=4)

    @pl.when(i < n_prompt_tiles)
    def _():
        scatter(up_ref)

    @pl.when(i == n_prompt_tiles)
    def _():
        scatter(us_ref)

    @pl.when(i == n_prompt_tiles + 1)
    def _():
        zero_scr[...] = jnp.zeros_like(zero_scr)

        def fill(e, start_not_wait):
            cnt = cnt_ref[e]
            pad = (-cnt) & (MOE_BLK - 1)
            off = ps_ref[e] + cnt

            def one(r, carry):
                cp = _row_copy(zero_scr.at[_tile_row(0)], xs_ref.at[_tile_row(off + r)], sem)
                if start_not_wait:
                    cp.start()
                else:
                    cp.wait()
                return carry

            lax.fori_loop(0, pad, one, 0)

        lax.fori_loop(0, N_EXPERTS, lambda e, c: (fill(e, True), c)[1], 0)
        lax.fori_loop(0, N_EXPERTS, lambda e, c: (fill(e, False), c)[1], 0)

        last = N_EXPERTS - 1
        used = (ps_ref[last] + cnt_ref[last] + (MOE_BLK - 1)) // MOE_BLK
        n_blocks = xs_ref.shape[0] // (MOE_BLK * SUBLANES)
        n_zero = zero_scr.shape[0] // SUBLANES

        def tail(start_not_wait):
            def one(b, carry):
                for r in range(0, MOE_BLK, n_zero):
                    cp = _row_copy(zero_scr, xs_ref.at[_tile_row(b * MOE_BLK + r, n_zero)], sem)
                    if start_not_wait:
                        cp.start()
                    else:
                        cp.wait()
                return carry

            lax.fori_loop(used, n_blocks, one, 0)

        tail(True)
        tail(False)


def _moe_dispatch(u_prompt, u_sample, dest, cnt, pstart):
    n_p = u_prompt.shape[0] // SUBLANES
    n_s = u_sample.shape[0] // SUBLANES
    n_tok = dest.shape[1]
    assert n_p % MOE_TILE == 0 and n_s <= MOE_TILE and n_tok == n_p + MOE_TILE
    npt = n_p // MOE_TILE
    cap = _moe_blocks(n_tok) * MOE_BLK
    grid_spec = pltpu.PrefetchScalarGridSpec(
        num_scalar_prefetch=2,
        grid=(npt + 2,),
        in_specs=[pl.BlockSpec((TOP_K, MOE_TILE), lambda i, c, p: (0, jnp.minimum(i, npt)),
                               memory_space=pltpu.SMEM),
                  pl.BlockSpec((MOE_TILE * SUBLANES, LANES), lambda i, c, p: (jnp.minimum(i, npt - 1), 0)),
                  pl.BlockSpec((n_s * SUBLANES, LANES), lambda i, c, p: (0, 0))],
        out_specs=pl.BlockSpec(memory_space=pl.ANY),
        scratch_shapes=[pltpu.VMEM((SUBLANES * SUBLANES, LANES), F32), pltpu.SemaphoreType.DMA(())],
    )
    return pl.pallas_call(
        functools.partial(_moe_dispatch_kernel, n_prompt_tiles=npt),
        grid_spec=grid_spec,
        out_shape=jax.ShapeDtypeStruct((cap * SUBLANES, LANES), F32),
        compiler_params=_params("arbitrary"),
        name="moe_dispatch",
    )(cnt, pstart, dest, u_prompt, u_sample)


def _moe_experts_kernel(be_ref, nu_ref, x_ref, wu_ref, bu_ref, wd_ref, bd_ref, y_ref, wu_bf, wd_bf):
    d_ff = wd_ref.shape[1]
    r = pl.program_id(0)
    active = r < nu_ref[0]
    new_expert = (r == 0) | (be_ref[r] != be_ref[jnp.maximum(r - 1, 0)])

    @pl.when(active & new_expert)
    def _():
        wu_bf[...] = wu_ref[0].astype(BF16)
        wd_bf[...] = wd_ref[0].astype(BF16)

    @pl.when(active)
    def _():
        x = _load_tile_rows(x_ref, MOE_BLK).astype(BF16)
        hdn = jnp.dot(x, wu_bf[...], preferred_element_type=F32) + bu_ref[0]
        glu = jnp.minimum(hdn[:, :d_ff], SWIGLU_LIMIT)
        lin = jnp.clip(hdn[:, d_ff:], -SWIGLU_LIMIT, SWIGLU_LIMIT)
        act = glu * _sigmoid(SWIGLU_ALPHA * glu) * (lin + 1.0)
        _store_tile_rows(y_ref, jnp.dot(act.astype(BF16), wd_bf[...], preferred_element_type=F32) + bd_ref[0])

    @pl.when(pl.program_id(0) >= nu_ref[0])
    def _():
        y_ref[...] = jnp.zeros_like(y_ref)


def _moe_experts(xs, blk_exp, n_used, w_up, b_up, w_down, b_down):
    blk = (MOE_BLK * SUBLANES, LANES)
    nb = xs.shape[0] // blk[0]
    last = lambda r, nu: jnp.minimum(r, nu[0] - 1)
    w_map = lambda r, be, nu: (be[last(r, nu)], 0, 0)
    x_map = lambda r, be, nu: (last(r, nu), 0)
    grid_spec = pltpu.PrefetchScalarGridSpec(
        num_scalar_prefetch=2,
        grid=(nb,),
        in_specs=[pl.BlockSpec(blk, x_map),
                  pl.BlockSpec((1,) + w_up.shape[1:], w_map), pl.BlockSpec((1,) + b_up.shape[1:], w_map),
                  pl.BlockSpec((1,) + w_down.shape[1:], w_map), pl.BlockSpec((1,) + b_down.shape[1:], w_map)],
        out_specs=pl.BlockSpec(blk, lambda r, be, nu: (r, 0)),
        scratch_shapes=[pltpu.VMEM(w_up.shape[1:], BF16), pltpu.VMEM(w_down.shape[1:], BF16)],
    )
    return pl.pallas_call(
        _moe_experts_kernel,
        grid_spec=grid_spec,
        out_shape=jax.ShapeDtypeStruct(xs.shape, F32),
        compiler_params=_params("arbitrary"),
        name="moe_experts",
    )(blk_exp, n_used, xs, w_up, b_up, w_down, b_down)


def _combine_kernel(dest_ref, dest_next_ref, h_ref, p_ref, gate_ref, ys_ref, wg_ref, wp_ref, gf_ref, o_ref,
                    y_scr, sem, *, final):
    tm = h_ref.shape[0]
    i = pl.program_id(0)
    slot = i % 2

    def gather(idx_ref, s):
        def start(t, carry):
            for j in range(TOP_K):
                _row_copy(ys_ref.at[_tile_row(idx_ref[j, t])], y_scr.at[s, j, _tile_row(t)],
                          sem.at[s]).start(priority=j % 2)
            return carry

        lax.fori_loop(0, tm, start, 0, unroll=4)

    @pl.when(i == 0)
    def _():
        gather(dest_ref, slot)

    @pl.when(i + 1 < pl.num_programs(0))
    def _():
        gather(dest_next_ref, 1 - slot)

    def wait(t, carry):
        for j in range(TOP_K):
            _row_copy(ys_ref.at[_tile_row(0)], y_scr.at[slot, 0, _tile_row(0)], sem.at[slot]).wait()
        return carry

    lax.fori_loop(0, tm, wait, 0, unroll=4)
    gate = gate_ref[...]
    h = h_ref[...]
    cols = []
    for c in range(SUBLANES):
        hc = h[:, c * LANES:(c + 1) * LANES]
        for j in range(TOP_K):
            hc = hc + gate[:, j:j + 1] * y_scr[slot, j, pl.ds(c, tm, stride=SUBLANES), :]
        cols.append(hc)
    h = jnp.concatenate(cols, axis=-1)
    emb_gate = _sigmoid(jnp.dot(h.astype(BF16), wg_ref[...], preferred_element_type=F32))
    emb = jnp.dot(p_ref[...].astype(BF16), wp_ref[...], preferred_element_type=F32)
    h = h + emb_gate * emb
    if final:
        ms = jnp.mean(h * h, axis=-1, keepdims=True)
        h = h * lax.rsqrt(ms + NORM_EPS) * gf_ref[...]
    o_ref[...] = h


def _combine(h1, p_emb, dest, gate_t, ys, w_gate, w_emb, g_final, final):
    m, d = h1.shape
    tm = _row_tile(m, 256)
    full = lambda a: pl.BlockSpec(a.shape, lambda i: (0,) * a.ndim)
    nt = m // tm
    return pl.pallas_call(
        functools.partial(_combine_kernel, final=final),
        grid=(nt,),
        in_specs=[pl.BlockSpec((TOP_K, tm), lambda i: (0, i), memory_space=pltpu.SMEM),
                  pl.BlockSpec((TOP_K, tm), lambda i: (0, jnp.minimum(i + 1, nt - 1)), memory_space=pltpu.SMEM),
                  pl.BlockSpec((tm, d), lambda i: (i, 0)),
                  pl.BlockSpec((tm, p_emb.shape[1]), lambda i: (i, 0)),
                  pl.BlockSpec((tm, TOP_K), lambda i: (i, 0)),
                  pl.BlockSpec(memory_space=pl.ANY),
                  full(w_gate), full(w_emb), full(g_final)],
        out_specs=pl.BlockSpec((tm, d), lambda i: (i, 0)),
        out_shape=jax.ShapeDtypeStruct((m, d), F32),
        scratch_shapes=[pltpu.VMEM((2, TOP_K, tm * SUBLANES, LANES), F32), pltpu.SemaphoreType.DMA((2,))],
        compiler_params=_params("arbitrary"),
        name="combine",
    )(dest, dest, h1, p_emb, gate_t, ys, w_gate, w_emb, g_final)


def kernel(x_prompt, x_sample, state_hgrn, cache_k, cache_v, cache_logf, page_table, p_prompt, p_sample, g_mix, w_in, hg_lb_logits, g_hg_out, fox_f_bias, g_fox_out, w_out, g_mlp, w_router, b_router, w_up, b_up, w_down, b_down, w_ple, w_ple_gate, g_final):
    depth = w_in.shape[0]
    bsz, seq, d = x_prompt.shape
    dec_b = x_sample.shape[0]
    n_p, n_s = bsz * seq, dec_b
    n_phys = cache_k.shape[1]
    lb_all = jnp.cumsum(jax.nn.softmax(hg_lb_logits.astype(F32), axis=0), axis=0)
    hp = x_prompt.reshape(n_p, d)
    hs = x_sample.reshape(n_s, d)
    n_main = 7 * HG_WIDTH
    outs = [[] for _ in range(8)]
    for i in range(depth):
        final = i == depth - 1
        w_main = w_in[i, :, :n_main].astype(BF16)
        w_ff = jnp.pad(w_in[i, :, n_main:], ((0, 0), (0, LANES - FOX_HEADS))).astype(BF16)
        f_bias = jnp.pad(fox_f_bias[i], (0, LANES - FOX_HEADS)).reshape(1, LANES)
        g_mix_i = g_mix[i].reshape(1, d)
        lb_i = lb_all[i].reshape(1, HG_WIDTH)
        g_hg = g_hg_out[i].reshape(1, HG_WIDTH)
        g_fox = g_fox_out[i].reshape(1, FOX_WIDTH)
        w_out_i = w_out[i].astype(BF16)
        g_mlp_i = g_mlp[i].reshape(1, d)
        w_r_t = w_router[i].T
        b_r = b_router[i].reshape(N_EXPERTS, 1)

        q, f, v, gt, fq, lf, fk, fv, kb, vb, c = _inproj(hp, g_mix_i, w_main, w_ff, lb_i, f_bias, seq)
        oh, sp = _hgrn_prompt(q, f, v, gt, g_hg, bsz, seq)
        of = _fox_prompt(fq, kb, vb, c, c.T, g_fox, bsz, seq)
        h1_p, u_p, idx_p, gate_p = _outproj(oh, of, hp, w_out_i, g_mlp_i, w_r_t, b_r)
        outs[0].append(sp)
        outs[1].append(fk.reshape(bsz, seq, FOX_HEADS, FOX_DH))
        outs[2].append(fv.reshape(bsz, seq, FOX_HEADS, FOX_DH))
        outs[3].append(lf.reshape(bsz, seq, FOX_HEADS))

        q, f, v, gt, fq, lf, fk, fv = _inproj(hs, g_mix_i, w_main, w_ff, lb_i, f_bias, 1)
        oh, ss = _hgrn_step(q, f, v, gt, g_hg, state_hgrn[i])
        kv_t = lambda a: jnp.transpose(a[i], (0, 2, 3, 1)).reshape(n_phys, FOX_WIDTH, PAGE_SIZE)
        of = _fox_decode(fq, fk, fv, lf, g_fox, kv_t(cache_k), kv_t(cache_v),
                         jnp.transpose(cache_logf[i], (0, 2, 1)), page_table)
        h1_s, u_s, idx_s, gate_s = _outproj(oh, of, hs, w_out_i, g_mlp_i, w_r_t, b_r)
        outs[4].append(ss)
        outs[5].append(fk.reshape(dec_b, 1, FOX_HEADS, FOX_DH))
        outs[6].append(fv.reshape(dec_b, 1, FOX_HEADS, FOX_DH))
        outs[7].append(lf.reshape(dec_b, 1, FOX_HEADS))

        n_tile = -(-(n_p + n_s) // MOE_TILE) * MOE_TILE
        idx_all = jnp.concatenate(
            [idx_p, idx_s, jnp.full((TOP_K, n_tile - n_p - n_s), N_EXPERTS, jnp.int32)], axis=1)
        dest, blk_exp, n_used, cnt, pstart = _moe_plan(idx_all)
        xs = _moe_dispatch(u_p, u_s, dest, cnt, pstart)
        ys = _moe_experts(xs, blk_exp, n_used, w_up[i], b_up[i].reshape(N_EXPERTS, 1, -1),
                          w_down[i], b_down[i].reshape(N_EXPERTS, 1, -1))
        w_pg = w_ple_gate[i].astype(BF16)
        w_p = w_ple[i].astype(BF16)
        g_fin = g_final.reshape(1, d)
        hp = _combine(h1_p, p_prompt[i].reshape(n_p, -1), dest[:, :n_p], gate_p.T, ys, w_pg, w_p, g_fin, final)
        hs = _combine(h1_s, p_sample[i].reshape(n_s, -1), dest[:, n_p:n_p + n_s], gate_s.T, ys, w_pg, w_p, g_fin,
                      final)
    st = [jnp.stack(o) for o in outs]
    return (hp.reshape(bsz, seq, d), hs.reshape(dec_b, 1, d), *st)
```

```python
import functools

import jax
import jax.numpy as jnp
from jax import lax
from jax.experimental import pallas as pl
from jax.experimental.pallas import tpu as pltpu

F32 = jnp.float32
BF16 = jnp.bfloat16

HG_HEADS = 4
HG_DK = 128
HG_DV = 128
HG_WIDTH = HG_HEADS * HG_DK
FOX_HEADS = 8
FOX_DH = 64
FOX_WIDTH = FOX_HEADS * FOX_DH
N_EXPERTS = 32
TOP_K = 4
PAGE_SIZE = 128
NORM_EPS = 1e-6
SWIGLU_ALPHA = 1.702
SWIGLU_LIMIT = 7.0

LANES = 128
SUBLANES = 8
NEG_BIG = -1e30
VMEM_LIMIT = 56 * 1024 * 1024

HG_CHUNK = 16
HG_BLOCK = 128
FOX_TQ = 512
FOX_TK = 512

def _params(*sem):
    return pltpu.CompilerParams(dimension_semantics=sem, vmem_limit_bytes=VMEM_LIMIT)


def _row_tile(m, pref):
    return pref if m % pref == 0 else m


def _split3(x):
    hi = x.astype(BF16)
    r1 = x - hi.astype(F32)
    mid = r1.astype(BF16)
    lo = (r1 - mid.astype(F32)).astype(BF16)
    return hi, mid, lo


def _dot01(mask_bf16, x):
    hi, mid, lo = _split3(x)
    d = lambda a: jnp.dot(mask_bf16, a, preferred_element_type=F32)
    return d(hi) + d(mid) + d(lo)


def _tri_lower(n, strict=False):
    r = lax.broadcasted_iota(jnp.int32, (n, n), 0)
    c = lax.broadcasted_iota(jnp.int32, (n, n), 1)
    m = (c < r) if strict else (c <= r)
    return jnp.where(m, 1.0, 0.0).astype(BF16)


def _sigmoid(x):
    return 1.0 / (1.0 + jnp.exp(-x))


def _silu(x):
    return x * _sigmoid(x)


def _log_sigmoid(x):
    return jnp.minimum(x, 0.0) - jnp.log1p(jnp.exp(-jnp.abs(x)))


def _inproj_kernel(x_ref, g_ref, w_ref, wf_ref, lb_ref, fb_ref,
                   q_ref, f_ref, i_ref, gt_ref, fq_ref, lf_ref, *rest, tiles_per_seq):
    x = x_ref[...]
    ms = jnp.mean(x * x, axis=-1, keepdims=True)
    u = (x * lax.rsqrt(ms + NORM_EPS) * g_ref[...]).astype(BF16)

    def proj(j):
        return jnp.dot(u, w_ref[:, j * HG_WIDTH:(j + 1) * HG_WIDTH], preferred_element_type=F32)

    q_ref[...] = _silu(proj(0))
    lb = lb_ref[...]
    f_ref[...] = lb + (1.0 - lb) * _sigmoid(proj(1))
    i_ref[...] = proj(2)
    gt_ref[...] = _silu(proj(3))
    fq_ref[...] = (proj(4) * (FOX_DH ** -0.5)).astype(BF16)
    fk = proj(5)
    fv = proj(6)
    ff = jnp.dot(u, wf_ref[...], preferred_element_type=F32) + fb_ref[...]
    lf = _log_sigmoid(ff)
    lf_ref[...] = lf[:, :FOX_HEADS]
    if not tiles_per_seq:
        fk_ref, fv_ref = rest
        fk_ref[...] = fk
        fv_ref[...] = fv
    else:
        kh_ref, vh_ref, kb_ref, vb_ref, c_ref, carry_ref = rest
        tm = x.shape[0]
        for h in range(FOX_HEADS):
            cols = slice(h * FOX_DH, (h + 1) * FOX_DH)
            kh_ref[pl.ds(h, tm, stride=FOX_HEADS), :] = fk[:, cols]
            vh_ref[pl.ds(h, tm, stride=FOX_HEADS), :] = fv[:, cols]
        kb_ref[...] = fk.astype(BF16)
        vb_ref[...] = fv.astype(BF16)

        @pl.when(pl.program_id(0) % tiles_per_seq == 0)
        def _():
            carry_ref[...] = jnp.zeros_like(carry_ref)

        c = _dot01(_tri_lower(tm), lf) + carry_ref[0:1, :]
        carry_ref[...] = jnp.broadcast_to(c[tm - 1:tm, :], carry_ref.shape)
        c_ref[...] = c[:, :FOX_HEADS]


def _inproj(x2d, g_mix, w_main, w_ff, lb, f_bias, seq_len):
    m, d = x2d.shape
    tm = _row_tile(m if seq_len == 1 else seq_len, 512)
    tiles_per_seq = 0 if seq_len == 1 else seq_len // tm
    row = lambda w: pl.BlockSpec((tm, w), lambda i: (i, 0))
    full = lambda a: pl.BlockSpec(a.shape, lambda i: (0,) * a.ndim)
    wide = jax.ShapeDtypeStruct((m, HG_WIDTH), F32)
    wide_bf = jax.ShapeDtypeStruct((m, HG_WIDTH), BF16)
    narrow = jax.ShapeDtypeStruct((m, FOX_HEADS), F32)
    out_specs = [row(HG_WIDTH)] * 5 + [row(FOX_HEADS)]
    out_shape = [wide, wide, wide, wide, wide_bf, narrow]
    scratch = []
    if not tiles_per_seq:
        out_specs += [row(HG_WIDTH), row(HG_WIDTH)]
        out_shape += [wide, wide]
    else:
        assert FOX_HEADS == SUBLANES
        heads = jax.ShapeDtypeStruct((m * FOX_HEADS, FOX_DH), F32)
        head_rows = pl.BlockSpec((tm * FOX_HEADS, FOX_DH), lambda i: (i, 0))
        out_specs += [head_rows, head_rows, row(HG_WIDTH), row(HG_WIDTH), row(FOX_HEADS)]
        out_shape += [heads, heads, wide_bf, wide_bf, narrow]
        scratch = [pltpu.VMEM((SUBLANES, LANES), F32)]
    return pl.pallas_call(
        functools.partial(_inproj_kernel, tiles_per_seq=tiles_per_seq),
        grid=(m // tm,),
        in_specs=[row(d), full(g_mix), full(w_main), full(w_ff), full(lb), full(f_bias)],
        out_specs=out_specs,
        out_shape=out_shape,
        scratch_shapes=scratch,
        compiler_params=_params("arbitrary"),
        name="inproj",
    )(x2d, g_mix, w_main, w_ff, lb, f_bias)


def _hgrn_kernel(q_ref, f_ref, v_ref, gt_ref, gn_ref, o_ref, s_ref, s_scr, b_scr):
    tb = q_ref.shape[0]

    @pl.when(pl.program_id(1) == 0)
    def _():
        s_scr[...] = jnp.zeros_like(s_scr)

    tri = _tri_lower(HG_BLOCK)
    row_id = lax.broadcasted_iota(jnp.int32, (HG_CHUNK, 1), 0)
    ones_kk = jnp.ones((HG_DK, HG_DK), BF16)

    def block_body(blk, carry):
        r_blk = pl.multiple_of(blk * HG_BLOCK, HG_BLOCK)
        b_scr[...] = _dot01(tri, jnp.log(f_ref[pl.ds(r_blk, HG_BLOCK), :]))

        def chunk_body(ci, carry2):
            r_loc = pl.multiple_of(ci * HG_CHUNK, HG_CHUNK)
            r0 = r_blk + r_loc
            for h in range(HG_HEADS):
                cols = slice(h * HG_DK, (h + 1) * HG_DK)
                q = q_ref[pl.ds(r0, HG_CHUNK), cols]
                k = 1.0 - f_ref[pl.ds(r0, HG_CHUNK), cols]
                v = v_ref[pl.ds(r0, HG_CHUNK), cols]
                bb = b_scr[pl.ds(r_loc, HG_CHUNK), cols]
                b_prev = bb[0:1, :] - jnp.log(f_ref[pl.ds(r0, 1), cols])
                b = bb - b_prev
                st_old = s_scr[h]
                o = lax.dot_general((q * jnp.exp(b)).astype(BF16), st_old.astype(BF16), (((1,), (1,)), ((), ())),
                                    preferred_element_type=F32)
                terms, first_row = [], []
                for s in range(HG_CHUNK):
                    t0 = (s // SUBLANES) * SUBLANES
                    d = jnp.where(row_id[t0:] >= s, b[t0:] - b[s:s + 1, :], NEG_BIG)
                    terms.append(jnp.exp(d) * q[t0:] * k[s:s + 1, :])
                    first_row.append(t0)
                a_all = jnp.dot(jnp.concatenate(terms, axis=0).astype(BF16), ones_kk, preferred_element_type=F32)
                o_rows = [o[t:t + SUBLANES] for t in range(0, HG_CHUNK, SUBLANES)]
                at = 0
                for s in range(HG_CHUNK):
                    for g in range(first_row[s] // SUBLANES, HG_CHUNK // SUBLANES):
                        o_rows[g] = o_rows[g] + a_all[at:at + SUBLANES] * v[s:s + 1, :]
                        at += SUBLANES
                o = jnp.concatenate(o_rows, axis=0)
                b_last = b[HG_CHUNK - 1:HG_CHUNK, :]
                kd = (k * jnp.exp(b_last - b)).astype(BF16)
                upd_t = lax.dot_general(v.astype(BF16), kd, (((0,), (0,)), ((), ())),
                                        preferred_element_type=F32)
                s_scr[h] = st_old * jnp.exp(b_last) + upd_t
                ms = jnp.mean(o * o, axis=-1, keepdims=True)
                o = o * lax.rsqrt(ms + NORM_EPS) * gn_ref[:, cols] * gt_ref[pl.ds(r0, HG_CHUNK), cols]
                o_ref[pl.ds(r0, HG_CHUNK), cols] = o.astype(BF16)
            return carry2

        return lax.fori_loop(0, HG_BLOCK // HG_CHUNK, chunk_body, carry, unroll=2)

    lax.fori_loop(0, tb // HG_BLOCK, block_body, 0)

    @pl.when(pl.program_id(1) == pl.num_programs(1) - 1)
    def _():
        for h in range(HG_HEADS):
            s_ref[0, h] = jnp.transpose(s_scr[h])


def _hgrn_prompt(q, f, v, gt, g_norm, batch, seq):
    tb = _row_tile(seq, 512)
    assert tb % HG_BLOCK == 0
    nt = seq // tb
    row = pl.BlockSpec((tb, HG_WIDTH), lambda b, t: (b * nt + t, 0))
    return pl.pallas_call(
        _hgrn_kernel,
        grid=(batch, nt),
        in_specs=[row, row, row, row, pl.BlockSpec((1, HG_WIDTH), lambda b, t: (0, 0))],
        out_specs=[row, pl.BlockSpec((1, HG_HEADS, HG_DK, HG_DV), lambda b, t: (b, 0, 0, 0))],
        out_shape=[jax.ShapeDtypeStruct((batch * seq, HG_WIDTH), BF16),
                   jax.ShapeDtypeStruct((batch, HG_HEADS, HG_DK, HG_DV), F32)],
        scratch_shapes=[pltpu.VMEM((HG_HEADS, HG_DK, HG_DV), F32), pltpu.VMEM((HG_BLOCK, HG_WIDTH), F32)],
        compiler_params=_params("arbitrary", "arbitrary"),
        name="hgrn_prompt",
    )(q, f, v, gt, g_norm)


def _fox_prompt_kernel(qt_ref, kt_ref, q_ref, k_ref, v_ref, cq_ref, ck_ref, gn_ref, o_ref, m_scr, l_scr, acc_scr):
    pair = pl.program_id(1)
    qi = qt_ref[pl.program_id(2)]
    ki = kt_ref[pl.program_id(2)]
    tq = q_ref.shape[0]
    tk = k_ref.shape[0]

    @pl.when(ki == 0)
    def _():
        m_scr[...] = jnp.full_like(m_scr, NEG_BIG)
        l_scr[...] = jnp.zeros_like(l_scr)
        acc_scr[...] = jnp.zeros_like(acc_scr)

    last_k = ((qi + 1) * tq - 1) // tk

    @pl.when(ki <= last_k)
    def _():
        lane = lax.broadcasted_iota(jnp.int32, (1, LANES), 1)
        first = lane < FOX_DH
        q = q_ref[...]
        k = k_ref[...]
        v = v_ref[...]
        q_pos = qi * tq + lax.broadcasted_iota(jnp.int32, (tq, tk), 0)
        k_pos = ki * tk + lax.broadcasted_iota(jnp.int32, (tq, tk), 1)
        causal = k_pos <= q_pos
        cq_all = cq_ref[...]
        ck_all = ck_ref[...]
        head_col = lax.broadcasted_iota(jnp.int32, (1, FOX_HEADS), 1)
        head_row = lax.broadcasted_iota(jnp.int32, (FOX_HEADS, 1), 0)
        pv = []
        for j in range(2):
            h = 2 * pair + j
            qh = jnp.where(first if j == 0 else ~first, q, jnp.zeros_like(q))
            s = lax.dot_general(qh, k, (((1,), (1,)), ((), ())), preferred_element_type=F32)
            cq = jnp.sum(jnp.where(head_col == h, cq_all, 0.0), axis=1, keepdims=True)
            ck = jnp.sum(jnp.where(head_row == h, ck_all, 0.0), axis=0, keepdims=True)
            s = jnp.where(causal, s + cq - ck, NEG_BIG)
            m_old = m_scr[j]
            m_new = jnp.maximum(m_old, jnp.max(s, axis=-1, keepdims=True))
            alpha = jnp.exp(m_old - m_new)
            p = jnp.exp(s - m_new)
            l_scr[j] = alpha * l_scr[j] + jnp.sum(p, axis=-1, keepdims=True)
            m_scr[j] = m_new
            pv.append((alpha, jnp.dot(p.astype(BF16), v, preferred_element_type=F32)))
        alpha = jnp.where(first, pv[0][0], pv[1][0])
        acc_scr[...] = alpha * acc_scr[...] + jnp.where(first, pv[0][1], pv[1][1])

    @pl.when(ki == last_k)
    def _():
        lane = lax.broadcasted_iota(jnp.int32, (1, LANES), 1)
        first = lane < FOX_DH
        o = acc_scr[...] / jnp.where(first, l_scr[0], l_scr[1])
        sq = o * o
        ss0 = jnp.sum(jnp.where(first, sq, 0.0), axis=-1, keepdims=True)
        ss1 = jnp.sum(jnp.where(first, 0.0, sq), axis=-1, keepdims=True)
        ms = jnp.where(first, ss0, ss1) * (1.0 / FOX_DH)
        o_ref[...] = (o * lax.rsqrt(ms + NORM_EPS) * gn_ref[...]).astype(BF16)


def _fox_prompt(q_bf, k_bf, v_bf, c_col, c_row, g_norm, batch, seq):
    tq = _row_tile(seq, FOX_TQ)
    tk = _row_tile(seq, FOX_TK)
    nq, nk = seq // tq, seq // tk
    pairs = FOX_HEADS // 2
    visit = [(i, j) for i in range(nq) for j in range(((i + 1) * tq - 1) // tk + 1)]
    q_tiles = jnp.asarray([i for i, _ in visit], jnp.int32)
    k_tiles = jnp.asarray([j for _, j in visit], jnp.int32)
    q_map = lambda b, p, s, qt, kt: (b * nq + qt[s], p)
    kv_map = lambda b, p, s, qt, kt: (b * nk + kt[s], p)
    grid_spec = pltpu.PrefetchScalarGridSpec(
        num_scalar_prefetch=2,
        grid=(batch, pairs, len(visit)),
        in_specs=[pl.BlockSpec((tq, LANES), q_map),
                  pl.BlockSpec((tk, LANES), kv_map),
                  pl.BlockSpec((tk, LANES), kv_map),
                  pl.BlockSpec((tq, FOX_HEADS), lambda b, p, s, qt, kt: (b * nq + qt[s], 0)),
                  pl.BlockSpec((FOX_HEADS, tk), lambda b, p, s, qt, kt: (0, b * nk + kt[s])),
                  pl.BlockSpec((1, LANES), lambda b, p, s, qt, kt: (0, p))],
        out_specs=pl.BlockSpec((tq, LANES), q_map),
        scratch_shapes=[pltpu.VMEM((2, tq, 1), F32), pltpu.VMEM((2, tq, 1), F32),
                        pltpu.VMEM((tq, LANES), F32)],
    )
    return pl.pallas_call(
        _fox_prompt_kernel,
        grid_spec=grid_spec,
        out_shape=jax.ShapeDtypeStruct((batch * seq, FOX_WIDTH), BF16),
        compiler_params=_params("arbitrary", "arbitrary", "arbitrary"),
        name="fox_prompt",
    )(q_tiles, k_tiles, q_bf, k_bf, v_bf, c_col, c_row, g_norm)


def _hgrn_step_kernel(q_ref, f_ref, v_ref, gt_ref, gn_ref, s_ref, o_ref, so_ref):
    rows = q_ref.shape[0]

    def col(row):
        return jnp.transpose(jnp.broadcast_to(row, (HG_DK, HG_DK)))

    for r in range(rows):
        for h in range(HG_HEADS):
            cols = slice(h * HG_DK, (h + 1) * HG_DK)
            f_col = col(f_ref[r:r + 1, cols])
            q_col = col(q_ref[r:r + 1, cols])
            s_new = f_col * s_ref[r, h] + (1.0 - f_col) * v_ref[r:r + 1, cols]
            so_ref[r, h] = s_new
            o = jnp.sum(s_new * q_col, axis=0, keepdims=True)
            ms = jnp.mean(o * o, axis=-1, keepdims=True)
            o = o * lax.rsqrt(ms + NORM_EPS) * gn_ref[:, cols] * gt_ref[r:r + 1, cols]
            o_ref[r:r + 1, cols] = o.astype(BF16)


def _hgrn_step(q, f, v, gt, g_norm, state):
    n = q.shape[0]
    rows = SUBLANES
    assert n % rows == 0
    row = pl.BlockSpec((rows, HG_WIDTH), lambda i: (i, 0))
    st = pl.BlockSpec((rows, HG_HEADS, HG_DK, HG_DV), lambda i: (i, 0, 0, 0))
    return pl.pallas_call(
        _hgrn_step_kernel,
        grid=(n // rows,),
        in_specs=[row, row, row, row, pl.BlockSpec((1, HG_WIDTH), lambda i: (0, 0)), st],
        out_specs=[row, st],
        out_shape=[jax.ShapeDtypeStruct((n, HG_WIDTH), BF16), jax.ShapeDtypeStruct(state.shape, F32)],
        compiler_params=_params("arbitrary"),
        name="hgrn_step",
    )(q, f, v, gt, g_norm, state)


FOX_PAGES_PER_STEP = 16


def _dot01_rhs(x, mask_bf16):
    hi, mid, lo = _split3(x)
    d = lambda a: jnp.dot(a, mask_bf16, preferred_element_type=F32)
    return d(hi) + d(mid) + d(lo)


def _fox_decode_kernel(pt_ref, q_ref, kn_ref, vn_ref, lfn_ref, gn_ref, *refs):
    del pt_ref
    npg = FOX_PAGES_PER_STEP
    k_refs, v_refs, lf_refs = refs[:npg], refs[npg:2 * npg], refs[2 * npg:3 * npg]
    o_ref, m_scr, l_scr, r_scr, acc_scr = refs[3 * npg:]
    c = pl.program_id(1)
    head = lax.broadcasted_iota(jnp.int32, (FOX_HEADS, FOX_WIDTH), 0)
    lane = lax.broadcasted_iota(jnp.int32, (FOX_HEADS, FOX_WIDTH), 1)
    own = lane // FOX_DH == head
    q_rows = jnp.where(own, q_ref[0].astype(F32), 0.0)
    lf_new = lfn_ref[0]

    @pl.when(c == 0)
    def _():
        k_new = kn_ref[0].astype(BF16).astype(F32)
        m_scr[...] = jnp.sum(q_rows * k_new, axis=1, keepdims=True)
        l_scr[...] = jnp.ones_like(l_scr)
        r_scr[...] = jnp.zeros_like(r_scr)
        acc_scr[...] = jnp.broadcast_to(vn_ref[0], acc_scr.shape)

    q_bf = q_rows.astype(BF16)
    rr = lax.broadcasted_iota(jnp.int32, (PAGE_SIZE, PAGE_SIZE), 0)
    cc = lax.broadcasted_iota(jnp.int32, (PAGE_SIZE, PAGE_SIZE), 1)
    later = jnp.where(rr > cc, 1.0, 0.0).astype(BF16)

    def stack(x):
        return jnp.concatenate([x[:, g * PAGE_SIZE:(g + 1) * PAGE_SIZE] for g in range(npg)], axis=0)

    def unstack(x):
        return jnp.concatenate([x[g * FOX_HEADS:(g + 1) * FOX_HEADS] for g in range(npg)], axis=1)

    def per_head(x, op):
        out = x[0:FOX_HEADS]
        for g in range(1, npg):
            out = op(out, x[g * FOX_HEADS:(g + 1) * FOX_HEADS])
        return out

    k_cat = jnp.concatenate([k_refs[g][0].astype(BF16) for g in range(npg)], axis=1)
    v_cat = jnp.concatenate([v_refs[g][0].astype(BF16) for g in range(npg)], axis=1)
    lf = jnp.concatenate([lf_refs[g][0] for g in range(npg)], axis=0)
    page_mass = jnp.sum(lf, axis=1, keepdims=True)
    run = r_scr[...]
    before = []
    for g in range(npg):
        before.append(run)
        run = run + page_mass[g * FOX_HEADS:(g + 1) * FOX_HEADS]
    r_scr[...] = run
    bias = _dot01_rhs(lf, later) + jnp.concatenate(before, axis=0) + jnp.concatenate([lf_new] * npg, axis=0)
    s = stack(jnp.dot(q_bf, k_cat, preferred_element_type=F32)) + bias

    m_old = m_scr[...]
    m_new = jnp.maximum(m_old, per_head(jnp.max(s, axis=1, keepdims=True), jnp.maximum))
    alpha = jnp.exp(m_old - m_new)
    p = jnp.exp(s - jnp.concatenate([m_new] * npg, axis=0))
    l_scr[...] = alpha * l_scr[...] + per_head(jnp.sum(p, axis=1, keepdims=True), jnp.add)
    m_scr[...] = m_new
    pv = lax.dot_general(unstack(p).astype(BF16), v_cat, (((1,), (1,)), ((), ())),
                         preferred_element_type=F32)
    acc_scr[...] = alpha * acc_scr[...] + pv

    @pl.when(c == pl.num_programs(1) - 1)
    def _():
        o = jnp.where(own, acc_scr[...] / l_scr[...], 0.0)
        ms = jnp.sum(o * o, axis=1, keepdims=True) * (1.0 / FOX_DH)
        o = o * lax.rsqrt(ms + NORM_EPS)
        o_ref[0] = (jnp.sum(o, axis=0, keepdims=True) * gn_ref[...]).astype(BF16)


def _fox_decode(q_bf, k_new, v_new, lf_new, g_norm, cache_kt, cache_vt, cache_lft, page_table):
    n, n_pages = page_table.shape
    npg = FOX_PAGES_PER_STEP
    assert n_pages % npg == 0
    n_chunks = n_pages // npg

    def page_map(g):
        return lambda b, c, pt: (pt[b, (n_chunks - 1 - c) * npg + (npg - 1 - g)], 0, 0)

    row3 = lambda h, w: pl.BlockSpec((1, h, w), lambda b, c, pt: (b, 0, 0))
    kv_specs = [pl.BlockSpec((1, FOX_WIDTH, PAGE_SIZE), page_map(g)) for g in range(npg)]
    lf_specs = [pl.BlockSpec((1, FOX_HEADS, PAGE_SIZE), page_map(g)) for g in range(npg)]
    stat = pltpu.VMEM((FOX_HEADS, 1), F32)
    grid_spec = pltpu.PrefetchScalarGridSpec(
        num_scalar_prefetch=1,
        grid=(n, n_chunks),
        in_specs=[row3(1, FOX_WIDTH), row3(1, FOX_WIDTH), row3(1, FOX_WIDTH), row3(FOX_HEADS, 1),
                  pl.BlockSpec((1, FOX_WIDTH), lambda b, c, pt: (0, 0))] + kv_specs + kv_specs + lf_specs,
        out_specs=row3(1, FOX_WIDTH),
        scratch_shapes=[stat, stat, stat, pltpu.VMEM((FOX_HEADS, FOX_WIDTH), F32)],
    )
    out = pl.pallas_call(
        _fox_decode_kernel,
        grid_spec=grid_spec,
        out_shape=jax.ShapeDtypeStruct((n, 1, FOX_WIDTH), BF16),
        compiler_params=_params("arbitrary", "arbitrary"),
        name="fox_decode",
    )(page_table, q_bf.reshape(n, 1, FOX_WIDTH), k_new.reshape(n, 1, FOX_WIDTH), v_new.reshape(n, 1, FOX_WIDTH),
      lf_new.reshape(n, FOX_HEADS, 1), g_norm, *([cache_kt] * npg), *([cache_vt] * npg), *([cache_lft] * npg))
    return out.reshape(n, FOX_WIDTH)


def _outproj_kernel(oh_ref, of_ref, x_ref, w_ref, g_ref, wr_ref, br_ref, h_ref, u_ref, idx_ref, gate_ref):
    mix_in = jnp.concatenate([oh_ref[...], of_ref[...]], axis=-1)
    h = x_ref[...] + jnp.dot(mix_in, w_ref[...], preferred_element_type=F32)
    h_ref[...] = h
    ms = jnp.mean(h * h, axis=-1, keepdims=True)
    u = h * lax.rsqrt(ms + NORM_EPS) * g_ref[...]
    _store_tile_rows(u_ref, u)
    vals = lax.dot_general(wr_ref[...], u, (((1,), (1,)), ((), ())), preferred_element_type=F32,
                           precision=lax.Precision.HIGHEST) + br_ref[...]
    expert = lax.broadcasted_iota(jnp.int32, vals.shape, 0)
    top_v, top_i = [], []
    for _ in range(TOP_K):
        m = jnp.max(vals, axis=0, keepdims=True)
        sel = jnp.min(jnp.where(vals == m, expert, N_EXPERTS), axis=0, keepdims=True)
        vals = jnp.where(expert == sel, -jnp.inf, vals)
        top_v.append(m)
        top_i.append(sel)
    ex = [jnp.exp(t - top_v[0]) for t in top_v]
    den = ex[0] + ex[1] + ex[2] + ex[3]
    idx_ref[...] = jnp.concatenate(top_i, axis=0)
    gate_ref[...] = jnp.concatenate([x / den for x in ex], axis=0)


def _outproj(oh, of, x, w_out, g_mlp, w_router_t, b_router):
    m, d = x.shape
    assert d == SUBLANES * LANES
    tm = _row_tile(m, 512)
    row = lambda w: pl.BlockSpec((tm, w), lambda i: (i, 0))
    full = lambda a: pl.BlockSpec(a.shape, lambda i: (0,) * a.ndim)
    colblk = pl.BlockSpec((TOP_K, tm), lambda i: (0, i))
    return pl.pallas_call(
        _outproj_kernel,
        grid=(m // tm,),
        in_specs=[row(HG_WIDTH), row(FOX_WIDTH), row(d), full(w_out), full(g_mlp), full(w_router_t),
                  full(b_router)],
        out_specs=[row(d), pl.BlockSpec((tm * SUBLANES, LANES), lambda i: (i, 0)), colblk, colblk],
        out_shape=[jax.ShapeDtypeStruct((m, d), F32), jax.ShapeDtypeStruct((m * SUBLANES, LANES), F32),
                   jax.ShapeDtypeStruct((TOP_K, m), jnp.int32), jax.ShapeDtypeStruct((TOP_K, m), F32)],
        compiler_params=_params("arbitrary"),
        name="outproj",
    )(oh, of, x, w_out, g_mlp, w_router_t, b_router)


MOE_TILE = 512
MOE_BLK = 256


def _moe_plan_kernel(idx_ref, dest_ref, be_ref, nu_ref, cnt_ref, ps_ref, cnt_scr, run_scr, ps_scr):
    phase = pl.program_id(0)
    step = pl.program_id(1)
    tm = idx_ref.shape[1]
    expert = lax.broadcasted_iota(jnp.int32, (N_EXPERTS, tm), 0)
    idx = idx_ref[...]
    onehot = [expert == idx[j:j + 1, :] for j in range(TOP_K)]

    @pl.when((phase == 0) & (step == 0))
    def _():
        cnt_scr[...] = jnp.zeros_like(cnt_scr)

    @pl.when(phase == 0)
    def _():
        tot = jnp.zeros((N_EXPERTS, 1), F32)
        for oh in onehot:
            tot = tot + jnp.sum(jnp.where(oh, 1.0, 0.0), axis=1, keepdims=True)
        cnt_scr[...] = cnt_scr[...] + tot

    @pl.when((phase == 1) & (step == 0))
    def _():
        cnt = cnt_scr[...]
        padded = jnp.floor((cnt + (MOE_BLK - 1)) * (1.0 / MOE_BLK)) * MOE_BLK
        pstart = _dot01(_tri_lower(N_EXPERTS, strict=True), padded)
        pend = pstart + padded
        ps_scr[...] = pstart
        run_scr[...] = jnp.zeros_like(run_scr)
        cnt_ref[...] = cnt.astype(jnp.int32)
        ps_ref[...] = pstart.astype(jnp.int32)
        nb = be_ref.shape[1]
        row0 = lax.broadcasted_iota(jnp.int32, (1, nb), 1).astype(F32) * MOE_BLK
        be = jnp.sum(jnp.where(pend[:, 0:1] <= row0, 1.0, 0.0), axis=0, keepdims=True)
        be_ref[...] = jnp.minimum(be, N_EXPERTS - 1).astype(jnp.int32)
        total = jnp.max(pend[:, 0:1], axis=0, keepdims=True)
        nu_ref[...] = jnp.broadcast_to(total * (1.0 / MOE_BLK), nu_ref.shape).astype(jnp.int32)

    @pl.when(phase == 1)
    def _():
        r = lax.broadcasted_iota(jnp.int32, (tm, tm), 0)
        c = lax.broadcasted_iota(jnp.int32, (tm, tm), 1)
        before = jnp.where(r < c, 1.0, 0.0).astype(BF16)
        run = run_scr[:, 0:1]
        base = ps_scr[:, 0:1]
        rows = []
        for oh in onehot:
            ohf = jnp.where(oh, 1.0, 0.0)
            prefix = jnp.dot(ohf.astype(BF16), before, preferred_element_type=F32)
            rows.append(jnp.sum(ohf * (base + run + prefix), axis=0, keepdims=True))
            run = run + jnp.sum(ohf, axis=1, keepdims=True)
        run_scr[...] = jnp.broadcast_to(run, run_scr.shape)
        dest_ref[...] = jnp.concatenate(rows, axis=0).astype(jnp.int32)


def _moe_blocks(n_tok):
    return n_tok * TOP_K // MOE_BLK + N_EXPERTS


def _moe_plan(idx_all):
    n_tok = idx_all.shape[1]
    nb = -(-_moe_blocks(n_tok) // LANES) * LANES
    lane_i32 = jax.ShapeDtypeStruct((N_EXPERTS, LANES), jnp.int32)
    const = lambda shape: pl.BlockSpec(shape, lambda p, s: (0, 0))
    dest, be, nu, cnt, ps = pl.pallas_call(
        _moe_plan_kernel,
        grid=(2, n_tok // MOE_TILE),
        in_specs=[pl.BlockSpec((TOP_K, MOE_TILE), lambda p, s: (0, s))],
        out_specs=[pl.BlockSpec((TOP_K, MOE_TILE), lambda p, s: (0, s * p)),
                   const((1, nb)), const((1, LANES)), const((N_EXPERTS, LANES)), const((N_EXPERTS, LANES))],
        out_shape=[jax.ShapeDtypeStruct((TOP_K, n_tok), jnp.int32), jax.ShapeDtypeStruct((1, nb), jnp.int32),
                   jax.ShapeDtypeStruct((1, LANES), jnp.int32), lane_i32, lane_i32],
        scratch_shapes=[pltpu.VMEM((N_EXPERTS, LANES), F32)] * 3,
        compiler_params=_params("arbitrary", "arbitrary"),
        name="moe_plan",
    )(idx_all)
    return dest, be[0], nu[0, :1], cnt[:, 0], ps[:, 0]


def _row_copy(src, dst, sem):
    return pltpu.make_async_copy(src, dst, sem)


def _tile_row(t, n=1):
    return pl.ds(pl.multiple_of(t * SUBLANES, SUBLANES), n * SUBLANES)


def _load_tile_rows(ref, n):
    return jnp.concatenate([ref[pl.ds(c, n, stride=SUBLANES), :] for c in range(SUBLANES)], axis=-1)


def _store_tile_rows(ref, val):
    n = val.shape[0]
    for c in range(SUBLANES):
        ref[pl.ds(c, n, stride=SUBLANES), :] = val[:, c * LANES:(c + 1) * LANES]


def _moe_dispatch_kernel(cnt_ref, ps_ref, dest_ref, up_ref, us_ref, xs_ref, zero_scr, sem, *, n_prompt_tiles):
    i = pl.program_id(0)

    def scatter(src_ref):
        n = src_ref.shape[0] // SUBLANES

        def start(t, carry):
            for j in range(TOP_K):
                _row_copy(src_ref.at[_tile_row(t)], xs_ref.at[_tile_row(dest_ref[j, t])],
                          sem).start(priority=j % 2)
            return carry

        def wait(t, carry):
            for j in range(TOP_K):
                _row_copy(src_ref.at[_tile_row(0)], xs_ref.at[_tile_row(0)], sem).wait()
            return carry

        lax.fori_loop(0, n, start, 0, unroll=4)
        lax.fori_loop(0, n, wait, 0, unroll=4)

    @pl.when(i < n_prompt_tiles)
    def _():
        scatter(up_ref)

    @pl.when(i == n_prompt_tiles)
    def _():
        scatter(us_ref)

    @pl.when(i == n_prompt_tiles + 1)
    def _():
        zero_scr[...] = jnp.zeros_like(zero_scr)

        def fill(e, start_not_wait):
            cnt = cnt_ref[e]
            pad = (-cnt) & (MOE_BLK - 1)
            off = ps_ref[e] + cnt

            def one(r, carry):
                cp = _row_copy(zero_scr.at[_tile_row(0)], xs_ref.at[_tile_row(off + r)], sem)
                if start_not_wait:
                    cp.start()
                else:
                    cp.wait()
                return carry

            lax.fori_loop(0, pad, one, 0)

        lax.fori_loop(0, N_EXPERTS, lambda e, c: (fill(e, True), c)[1], 0)
        lax.fori_loop(0, N_EXPERTS, lambda e, c: (fill(e, False), c)[1], 0)

        last = N_EXPERTS - 1
        used = (ps_ref[last] + cnt_ref[last] + (MOE_BLK - 1)) // MOE_BLK
        n_blocks = xs_ref.shape[0] // (MOE_BLK * SUBLANES)
        n_zero = zero_scr.shape[0] // SUBLANES

        def tail(start_not_wait):
            def one(b, carry):
                for r in range(0, MOE_BLK, n_zero):
                    cp = _row_copy(zero_scr, xs_ref.at[_tile_row(b * MOE_BLK + r, n_zero)], sem)
                    if start_not_wait:
                        cp.start()
                    else:
                        cp.wait()
                return carry

            lax.fori_loop(used, n_blocks, one, 0)

        tail(True)
        tail(False)


def _moe_dispatch(u_prompt, u_sample, dest, cnt, pstart):
    n_p = u_prompt.shape[0] // SUBLANES
    n_s = u_sample.shape[0] // SUBLANES
    n_tok = dest.shape[1]
    assert n_p % MOE_TILE == 0 and n_s <= MOE_TILE and n_tok == n_p + MOE_TILE
    npt = n_p // MOE_TILE
    cap = _moe_blocks(n_tok) * MOE_BLK
    grid_spec = pltpu.PrefetchScalarGridSpec(
        num_scalar_prefetch=2,
        grid=(npt + 2,),
        in_specs=[pl.BlockSpec((TOP_K, MOE_TILE), lambda i, c, p: (0, jnp.minimum(i, npt)),
                               memory_space=pltpu.SMEM),
                  pl.BlockSpec((MOE_TILE * SUBLANES, LANES), lambda i, c, p: (jnp.minimum(i, npt - 1), 0)),
                  pl.BlockSpec((n_s * SUBLANES, LANES), lambda i, c, p: (0, 0))],
        out_specs=pl.BlockSpec(memory_space=pl.ANY),
        scratch_shapes=[pltpu.VMEM((SUBLANES * SUBLANES, LANES), F32), pltpu.SemaphoreType.DMA(())],
    )
    return pl.pallas_call(
        functools.partial(_moe_dispatch_kernel, n_prompt_tiles=npt),
        grid_spec=grid_spec,
        out_shape=jax.ShapeDtypeStruct((cap * SUBLANES, LANES), F32),
        compiler_params=_params("arbitrary"),
        name="moe_dispatch",
    )(cnt, pstart, dest, u_prompt, u_sample)


def _moe_experts_kernel(be_ref, nu_ref, x_ref, wu_ref, bu_ref, wd_ref, bd_ref, y_ref, wu_bf, wd_bf):
    d_ff = wd_ref.shape[1]
    r = pl.program_id(0)
    active = r < nu_ref[0]
    new_expert = (r == 0) | (be_ref[r] != be_ref[jnp.maximum(r - 1, 0)])

    @pl.when(active & new_expert)
    def _():
        wu_bf[...] = wu_ref[0].astype(BF16)
        wd_bf[...] = wd_ref[0].astype(BF16)

    @pl.when(active)
    def _():
        x = _load_tile_rows(x_ref, MOE_BLK).astype(BF16)
        hdn = jnp.dot(x, wu_bf[...], preferred_element_type=F32) + bu_ref[0]
        glu = jnp.minimum(hdn[:, :d_ff], SWIGLU_LIMIT)
        lin = jnp.clip(hdn[:, d_ff:], -SWIGLU_LIMIT, SWIGLU_LIMIT)
        act = glu * _sigmoid(SWIGLU_ALPHA * glu) * (lin + 1.0)
        _store_tile_rows(y_ref, jnp.dot(act.astype(BF16), wd_bf[...], preferred_element_type=F32) + bd_ref[0])

    @pl.when(pl.program_id(0) >= nu_ref[0])
    def _():
        y_ref[...] = jnp.zeros_like(y_ref)


def _moe_experts(xs, blk_exp, n_used, w_up, b_up, w_down, b_down):
    blk = (MOE_BLK * SUBLANES, LANES)
    nb = xs.shape[0] // blk[0]
    last = lambda r, nu: jnp.minimum(r, nu[0] - 1)
    w_map = lambda r, be, nu: (be[last(r, nu)], 0, 0)
    x_map = lambda r, be, nu: (last(r, nu), 0)
    grid_spec = pltpu.PrefetchScalarGridSpec(
        num_scalar_prefetch=2,
        grid=(nb,),
        in_specs=[pl.BlockSpec(blk, x_map),
                  pl.BlockSpec((1,) + w_up.shape[1:], w_map), pl.BlockSpec((1,) + b_up.shape[1:], w_map),
                  pl.BlockSpec((1,) + w_down.shape[1:], w_map), pl.BlockSpec((1,) + b_down.shape[1:], w_map)],
        out_specs=pl.BlockSpec(blk, lambda r, be, nu: (r, 0)),
        scratch_shapes=[pltpu.VMEM(w_up.shape[1:], BF16), pltpu.VMEM(w_down.shape[1:], BF16)],
    )
    return pl.pallas_call(
        _moe_experts_kernel,
        grid_spec=grid_spec,
        out_shape=jax.ShapeDtypeStruct(xs.shape, F32),
        compiler_params=_params("arbitrary"),
        name="moe_experts",
    )(blk_exp, n_used, xs, w_up, b_up, w_down, b_down)


def _combine_kernel(dest_ref, dest_next_ref, h_ref, p_ref, gate_ref, ys_ref, wg_ref, wp_ref, gf_ref, o_ref,
                    y_scr, sem, *, final):
    tm = h_ref.shape[0]
    i = pl.program_id(0)
    slot = i % 2

    def gather(idx_ref, s):
        def start(t, carry):
            for j in range(TOP_K):
                _row_copy(ys_ref.at[_tile_row(idx_ref[j, t])], y_scr.at[s, j, _tile_row(t)],
                          sem.at[s]).start(priority=j % 2)
            return carry

        lax.fori_loop(0, tm, start, 0, unroll=4)

    @pl.when(i == 0)
    def _():
        gather(dest_ref, slot)

    @pl.when(i + 1 < pl.num_programs(0))
    def _():
        gather(dest_next_ref, 1 - slot)

    def wait(t, carry):
        for j in range(TOP_K):
            _row_copy(ys_ref.at[_tile_row(0)], y_scr.at[slot, 0, _tile_row(0)], sem.at[slot]).wait()
        return carry

    lax.fori_loop(0, tm, wait, 0, unroll=4)
    gate = gate_ref[...]
    h = h_ref[...]
    cols = []
    for c in range(SUBLANES):
        hc = h[:, c * LANES:(c + 1) * LANES]
        for j in range(TOP_K):
            hc = hc + gate[:, j:j + 1] * y_scr[slot, j, pl.ds(c, tm, stride=SUBLANES), :]
        cols.append(hc)
    h = jnp.concatenate(cols, axis=-1)
    emb_gate = _sigmoid(jnp.dot(h.astype(BF16), wg_ref[...], preferred_element_type=F32))
    emb = jnp.dot(p_ref[...].astype(BF16), wp_ref[...], preferred_element_type=F32)
    h = h + emb_gate * emb
    if final:
        ms = jnp.mean(h * h, axis=-1, keepdims=True)
        h = h * lax.rsqrt(ms + NORM_EPS) * gf_ref[...]
    o_ref[...] = h


def _combine(h1, p_emb, dest, gate_t, ys, w_gate, w_emb, g_final, final):
    m, d = h1.shape
    tm = _row_tile(m, 256)
    full = lambda a: pl.BlockSpec(a.shape, lambda i: (0,) * a.ndim)
    nt = m // tm
    return pl.pallas_call(
        functools.partial(_combine_kernel, final=final),
        grid=(nt,),
        in_specs=[pl.BlockSpec((TOP_K, tm), lambda i: (0, i), memory_space=pltpu.SMEM),
                  pl.BlockSpec((TOP_K, tm), lambda i: (0, jnp.minimum(i + 1, nt - 1)), memory_space=pltpu.SMEM),
                  pl.BlockSpec((tm, d), lambda i: (i, 0)),
                  pl.BlockSpec((tm, p_emb.shape[1]), lambda i: (i, 0)),
                  pl.BlockSpec((tm, TOP_K), lambda i: (i, 0)),
                  pl.BlockSpec(memory_space=pl.ANY),
                  full(w_gate), full(w_emb), full(g_final)],
        out_specs=pl.BlockSpec((tm, d), lambda i: (i, 0)),
        out_shape=jax.ShapeDtypeStruct((m, d), F32),
        scratch_shapes=[pltpu.VMEM((2, TOP_K, tm * SUBLANES, LANES), F32), pltpu.SemaphoreType.DMA((2,))],
        compiler_params=_params("arbitrary"),
        name="combine",
    )(dest, dest, h1, p_emb, gate_t, ys, w_gate, w_emb, g_final)


def kernel(x_prompt, x_sample, state_hgrn, cache_k, cache_v, cache_logf, page_table, p_prompt, p_sample, g_mix, w_in, hg_lb_logits, g_hg_out, fox_f_bias, g_fox_out, w_out, g_mlp, w_router, b_router, w_up, b_up, w_down, b_down, w_ple, w_ple_gate, g_final):
    depth = w_in.shape[0]
    bsz, seq, d = x_prompt.shape
    dec_b = x_sample.shape[0]
    n_p, n_s = bsz * seq, dec_b
    n_phys = cache_k.shape[1]
    lb_all = jnp.cumsum(jax.nn.softmax(hg_lb_logits.astype(F32), axis=0), axis=0)
    hp = x_prompt.reshape(n_p, d)
    hs = x_sample.reshape(n_s, d)
    n_main = 7 * HG_WIDTH
    outs = [[] for _ in range(8)]
    for i in range(depth):
        final = i == depth - 1
        w_main = w_in[i, :, :n_main].astype(BF16)
        w_ff = jnp.pad(w_in[i, :, n_main:], ((0, 0), (0, LANES - FOX_HEADS))).astype(BF16)
        f_bias = jnp.pad(fox_f_bias[i], (0, LANES - FOX_HEADS)).reshape(1, LANES)
        g_mix_i = g_mix[i].reshape(1, d)
        lb_i = lb_all[i].reshape(1, HG_WIDTH)
        g_hg = g_hg_out[i].reshape(1, HG_WIDTH)
        g_fox = g_fox_out[i].reshape(1, FOX_WIDTH)
        w_out_i = w_out[i].astype(BF16)
        g_mlp_i = g_mlp[i].reshape(1, d)
        w_r_t = w_router[i].T
        b_r = b_router[i].reshape(N_EXPERTS, 1)

        q, f, v, gt, fq, lf, fk, fv, kb, vb, c = _inproj(hp, g_mix_i, w_main, w_ff, lb_i, f_bias, seq)
        oh, sp = _hgrn_prompt(q, f, v, gt, g_hg, bsz, seq)
        of = _fox_prompt(fq, kb, vb, c, c.T, g_fox, bsz, seq)
        h1_p, u_p, idx_p, gate_p = _outproj(oh, of, hp, w_out_i, g_mlp_i, w_r_t, b_r)
        outs[0].append(sp)
        outs[1].append(fk.reshape(bsz, seq, FOX_HEADS, FOX_DH))
        outs[2].append(fv.reshape(bsz, seq, FOX_HEADS, FOX_DH))
        outs[3].append(lf.reshape(bsz, seq, FOX_HEADS))

        q, f, v, gt, fq, lf, fk, fv = _inproj(hs, g_mix_i, w_main, w_ff, lb_i, f_bias, 1)
        oh, ss = _hgrn_step(q, f, v, gt, g_hg, state_hgrn[i])
        kv_t = lambda a: jnp.transpose(a[i], (0, 2, 3, 1)).reshape(n_phys, FOX_WIDTH, PAGE_SIZE)
        of = _fox_decode(fq, fk, fv, lf, g_fox, kv_t(cache_k), kv_t(cache_v),
                         jnp.transpose(cache_logf[i], (0, 2, 1)), page_table)
        h1_s, u_s, idx_s, gate_s = _outproj(oh, of, hs, w_out_i, g_mlp_i, w_r_t, b_r)
        outs[4].append(ss)
        outs[5].append(fk.reshape(dec_b, 1, FOX_HEADS, FOX_DH))
        outs[6].append(fv.reshape(dec_b, 1, FOX_HEADS, FOX_DH))
        outs[7].append(lf.reshape(dec_b, 1, FOX_HEADS))

        n_tile = -(-(n_p + n_s) // MOE_TILE) * MOE_TILE
        idx_all = jnp.concatenate(
            [idx_p, idx_s, jnp.full((TOP_K, n_tile - n_p - n_s), N_EXPERTS, jnp.int32)], axis=1)
        dest, blk_exp, n_used, cnt, pstart = _moe_plan(idx_all)
        xs = _moe_dispatch(u_p, u_s, dest, cnt, pstart)
        ys = _moe_experts(xs, blk_exp, n_used, w_up[i], b_up[i].reshape(N_EXPERTS, 1, -1),
                          w_down[i], b_down[i].reshape(N_EXPERTS, 1, -1))
        w_pg = w_ple_gate[i].astype(BF16)
        w_p = w_ple[i].astype(BF16)
        g_fin = g_final.reshape(1, d)
        hp = _combine(h1_p, p_prompt[i].reshape(n_p, -1), dest[:, :n_p], gate_p.T, ys, w_pg, w_p, g_fin, final)
        hs = _combine(h1_s, p_sample[i].reshape(n_s, -1), dest[:, n_p:n_p + n_s], gate_s.T, ys, w_pg, w_p, g_fin,
                      final)
    st = [jnp.stack(o) for o in outs]
    return (hp.reshape(bsz, seq, d), hs.reshape(dec_b, 1, d), *st)
```
